```python
import jax
import jax.numpy as jnp
from jax import lax
import numpy as np

D_MODEL = 1024
BATCH = 2
SEQ = 16384
DEPTH = 4
DEC_BATCH = 16
DEC_SEQ = 32
PAST_LEN = 4096

CHUNK = 64
NORM_EPS = 1e-6
N_BRANCH = 3
POOL_WINDOWS = (2, 4, 8, 16)
POOL_GROUP = D_MODEL // 8
POOL_WIDTH = POOL_GROUP * len(POOL_WINDOWS)
POOL_HIST = max(POOL_WINDOWS) - 1
HEAD_DIM = 64
N_HEADS = D_MODEL // 128
N_KV = 2
Q_GROUP = N_HEADS // N_KV
ATTN_WIDTH = N_HEADS * HEAD_DIM
KV_WIDTH = N_KV * HEAD_DIM
WINDOW = 128
WIN_CHUNKS = WINDOW // CHUNK
ATTN_SCALE = HEAD_DIM ** -0.5
RW_HD = 64
RW_HEADS = D_MODEL // 128
RW_WIDTH = RW_HEADS * RW_HD
RW_DECAY_LORA = 64
RW_A_LORA = 64
RW_GATE_LORA = 160
RW_IN = 3 * RW_WIDTH + RW_DECAY_LORA + RW_A_LORA + RW_GATE_LORA
RW_GN_EPS = 64e-5
OFF_Q = POOL_WIDTH
OFF_K = OFF_Q + ATTN_WIDTH
OFF_V = OFF_K + KV_WIDTH
OFF_RW = OFF_V + KV_WIDTH
OFF_GATE = OFF_RW + RW_IN
IN_COLS = OFF_GATE + N_BRANCH * D_MODEL
D_FF = 11 * D_MODEL // 4
N_EXPERTS = 8
TOP_K = 2
D_FF_EXPERT = D_FF // 2
N_DENSE = (DEPTH + 1) // 2
N_MOE = DEPTH // 2

kernel_name = 'hybrid_streaming_encoder_step'

F32 = jnp.float32


def _rms_norm(x, g):
    xf = x.astype(F32)
    y = xf * lax.rsqrt(jnp.mean(xf * xf, axis=-1, keepdims=True) + NORM_EPS)
    return (y * g.astype(F32)).astype(x.dtype)


def _modulate(xn, shift, scale):
    return xn * (1 + scale[:, None, :]) + shift[:, None, :]


def _sink_softmax(s, sink):
    m = jnp.maximum(jnp.max(s, axis=-1, keepdims=True), sink)
    p = jnp.exp(s - m)
    return p / (jnp.sum(p, axis=-1, keepdims=True) + jnp.exp(sink - m))


def _pool_branch(u, hist, pos0, w_grp, scale):
    B, T, _ = u.shape
    up = jnp.concatenate([hist, u], axis=1).astype(F32)
    cs = jnp.concatenate([jnp.zeros((B, 1, POOL_WIDTH), F32), jnp.cumsum(up, axis=1)], axis=1)
    pos = pos0 + jnp.arange(T, dtype=jnp.int32)
    base = POOL_HIST + 1
    means = []
    for g, w in enumerate(POOL_WINDOWS):
        sl = slice(g * POOL_GROUP, (g + 1) * POOL_GROUP)
        win_sum = cs[:, base:base + T, sl] - cs[:, base - w:base - w + T, sl]
        cnt = jnp.minimum(pos + 1, w).astype(F32)
        means.append(win_sum / cnt[None, :, None])
    d = jnp.concatenate(means, axis=-1) - u.astype(F32)
    d = d.reshape(B, T, len(POOL_WINDOWS), POOL_GROUP)
    y = jnp.einsum('btgc,gce->btge', d, w_grp.astype(F32)).reshape(B, T, POOL_WIDTH)
    return (y * scale).astype(u.dtype), up[:, -POOL_HIST:].astype(u.dtype)


def _attn_banded(q, k, v, sink):
    B, T = q.shape[:2]
    nc = T // CHUNK
    span = (WIN_CHUNKS + 1) * CHUNK
    qb = q.reshape(B, nc, CHUNK, N_KV, Q_GROUP, HEAD_DIM)
    pad = ((0, 0), (WIN_CHUNKS * CHUNK, 0), (0, 0), (0, 0))
    kp = jnp.pad(k, pad).reshape(B, nc + WIN_CHUNKS, CHUNK, N_KV, HEAD_DIM)
    vp = jnp.pad(v, pad).reshape(B, nc + WIN_CHUNKS, CHUNK, N_KV, HEAD_DIM)
    kb = jnp.concatenate([kp[:, j:j + nc] for j in range(WIN_CHUNKS + 1)], axis=2)
    vb = jnp.concatenate([vp[:, j:j + nc] for j in range(WIN_CHUNKS + 1)], axis=2)
    key_blk = jnp.arange(nc)[:, None] - WIN_CHUNKS + jnp.arange(span)[None, :] // CHUNK
    valid = key_blk >= 0
    s = jnp.einsum('bnqkgd,bnskd->bnkgqs', qb, kb, preferred_element_type=F32) * ATTN_SCALE
    s = jnp.where(valid[None, :, None, None, None, :], s, -jnp.inf)
    p = _sink_softmax(s, sink.reshape(N_KV, Q_GROUP)[None, None, :, :, None, None])
    o = jnp.einsum('bnkgqs,bnskd->bnqkgd', p.astype(v.dtype), vb)
    return o.reshape(B, T, ATTN_WIDTH)


def _attn_cached(q, k, v, ck, cv, sink):
    B, T = q.shape[:2]
    kf = jnp.concatenate([ck, k], axis=1)
    vf = jnp.concatenate([cv, v], axis=1)
    qh = q.reshape(B, T, N_KV, Q_GROUP, HEAD_DIM)
    s = jnp.einsum('btkgd,bskd->bkgts', qh, kf, preferred_element_type=F32) * ATTN_SCALE
    p = _sink_softmax(s, sink.reshape(N_KV, Q_GROUP)[None, :, :, None, None])
    o = jnp.einsum('bkgts,bskd->btkgd', p.astype(vf.dtype), vf)
    return o.reshape(B, T, ATTN_WIDTH)


def _rwkv_branch(p, shift_hist, wkv0, W, l):
    B, T, _ = p.shape
    prev = jnp.concatenate([shift_hist, p[:, :-1]], axis=1)
    pm = p + (prev - p) * W['rw_mu'][l]
    o1 = 3 * RW_WIDTH + RW_DECAY_LORA
    r, k, v, wd, ad, gd = jnp.split(pm, [RW_WIDTH, 2 * RW_WIDTH, 3 * RW_WIDTH, o1, o1 + RW_A_LORA], axis=-1)
    w_log = -jax.nn.softplus(-(W['rw_w0'][l] + jnp.tanh(wd) @ W['rw_w2'][l])) - 0.5
    decay = jnp.exp(-jnp.exp(w_log.astype(F32)))
    a = jax.nn.sigmoid(W['rw_a0'][l] + ad @ W['rw_a2'][l])
    g = jax.nn.sigmoid(gd) @ W['rw_g2'][l]
    heads = lambda t: t.astype(F32).reshape(B, T, RW_HEADS, RW_HD)
    kk = heads(k * W['rw_k_k'][l])
    kk = kk / jnp.maximum(jnp.sqrt(jnp.sum(kk * kk, axis=-1, keepdims=True)), 1e-12)
    k = k * (1 + (a - 1) * W['rw_k_a'][l])
    rh, kh, vh, ah, wh = heads(r), heads(k), heads(v), heads(a), heads(decay)

    def step(S, inp):
        r_t, w_t, k_t, v_t, kk_t, a_t = inp
        sa = jnp.einsum('bhvk,bhk->bhv', S, -kk_t)
        S = (S * w_t[:, :, None, :] + sa[..., None] * (kk_t * a_t)[:, :, None, :]
             + v_t[..., None] * k_t[:, :, None, :])
        return S, jnp.einsum('bhvk,bhk->bhv', S, r_t)

    xs = tuple(jnp.moveaxis(t, 1, 0) for t in (rh, wh, kh, vh, kk, ah))
    wkv, y = lax.scan(step, wkv0.astype(F32), xs)
    y = jnp.moveaxis(y, 0, 1)
    mu = jnp.mean(y, axis=-1, keepdims=True)
    var = jnp.mean(jnp.square(y - mu), axis=-1, keepdims=True)
    y = ((y - mu) * lax.rsqrt(var + RW_GN_EPS)).reshape(B, T, RW_WIDTH) * W['rw_ln_w'][l] + W['rw_ln_b'][l]
    bonus = jnp.sum(rh * kh * W['rw_r_k'][l].astype(F32), axis=-1, keepdims=True) * vh
    y = (y + bonus.reshape(B, T, RW_WIDTH)) * g
    return y.astype(p.dtype), p[:, -1:], wkv


def _swiglu(h, w1, w3, w2):
    return (jax.nn.silu(h @ w1) * (h @ w3)) @ w2


def _moe(h, rw, rb, w1, w3, w2):
    logits = jnp.einsum('btd,de->bte', h, rw, preferred_element_type=F32) + rb.astype(F32)
    probs = jax.nn.softmax(logits, axis=-1)
    top_p, top_i = lax.top_k(probs, TOP_K)
    top_p = top_p / jnp.sum(top_p, axis=-1, keepdims=True)
    gate = jnp.sum(jax.nn.one_hot(top_i, N_EXPERTS, dtype=F32) * top_p[..., None], axis=-2)
    y = jnp.zeros(h.shape, F32)
    for e in range(N_EXPERTS):
        y = y + gate[..., e:e + 1] * _swiglu(h, w1[e], w3[e], w2[e]).astype(F32)
    return y.astype(h.dtype)


def _layer(x, c, pos0, l, W, hist_pool, hist_shift, wkv0, ck, cv):
    B, T, _ = x.shape
    mod = jax.nn.silu(c) @ W['ada_w'][l] + W['ada_b'][l]
    sh1, sc1, ga1, sh2, sc2, ga2 = jnp.split(mod, 6, axis=-1)
    h = _modulate(_rms_norm(x, W['norm1_g'][l]), sh1, sc1)
    z = h @ W['w_in'][l]
    o_pool, new_pool = _pool_branch(z[..., :OFF_Q], hist_pool, pos0, W['pool_w'][l], W['pool_scale'][l])
    q = _rms_norm(z[..., OFF_Q:OFF_K].reshape(B, T, N_HEADS, HEAD_DIM), W['q_norm_g'][l])
    k = _rms_norm(z[..., OFF_K:OFF_V].reshape(B, T, N_KV, HEAD_DIM), W['k_norm_g'][l])
    v = z[..., OFF_V:OFF_RW].reshape(B, T, N_KV, HEAD_DIM)
    sink = W['attn_sink'][l].astype(F32)
    if ck is None:
        o_attn = _attn_banded(q, k, v, sink)
        new_k, new_v = k[:, -WINDOW:], v[:, -WINDOW:]
    else:
        o_attn = _attn_cached(q, k, v, ck, cv, sink)
        new_k, new_v = k, v
    o_rw, new_shift, new_wkv = _rwkv_branch(z[..., OFF_RW:OFF_GATE], hist_shift, wkv0, W, l)
    gates = jax.nn.sigmoid(z[..., OFF_GATE:].reshape(B, T, N_BRANCH, D_MODEL))
    merged = (gates[:, :, 0] * (o_pool @ W['proj_pool'][l])
              + gates[:, :, 1] * (o_attn @ W['proj_attn'][l])
              + gates[:, :, 2] * (o_rw @ W['proj_rwkv'][l]))
    x = x + ga1[:, None, :] * (merged @ W['w_out'][l])
    h2 = _modulate(_rms_norm(x, W['norm2_g'][l]), sh2, sc2)
    i = l // 2
    if l % 2 == 0:
        f = _swiglu(h2, W['ffn_w1'][i], W['ffn_w3'][i], W['ffn_w2'][i])
    else:
        f = _moe(h2, W['router_w'][i], W['router_b'][i], W['moe_w1'][i], W['moe_w3'][i], W['moe_w2'][i])
    x = x + ga2[:, None, :] * f
    return x, (new_k, new_v, new_pool, new_shift, new_wkv)


def _trunk(x, c, W, pos0, cache_k=None, cache_v=None, st_pool=None, st_shift=None, st_wkv=None):
    B = x.shape[0]
    outs = ([], [], [], [], [])
    for l in range(DEPTH):
        if cache_k is None:
            hp = jnp.zeros((B, POOL_HIST, POOL_WIDTH), x.dtype)
            hs = jnp.zeros((B, 1, RW_IN), x.dtype)
            s0 = jnp.zeros((B, RW_HEADS, RW_HD, RW_HD), F32)
            ck = cv = None
        else:
            hp, hs, s0, ck, cv = st_pool[l], st_shift[l], st_wkv[l], cache_k[l], cache_v[l]
        x, new = _layer(x, c, pos0, l, W, hp, hs, s0, ck, cv)
        for lst, arr in zip(outs, new):
            lst.append(arr)
    return x, [jnp.stack(lst) for lst in outs]


def setup_inputs(seed: int = 0) -> dict:
    key = jax.random.key(seed)
    ks = iter(jax.random.split(key, 64))

    def nrm(shape, scale=1.0):
        return jax.random.normal(next(ks), shape, F32) * scale

    def uni(shape, lo, hi):
        return jax.random.uniform(next(ks), shape, F32, lo, hi)

    L = DEPTH
    cache_len = min(WINDOW, PAST_LEN)
    return {
        'x_prompt': nrm((BATCH, SEQ, D_MODEL)),
        'x_sample': nrm((DEC_BATCH, DEC_SEQ, D_MODEL)),
        'c_prompt': nrm((BATCH, D_MODEL)),
        'c_sample': nrm((DEC_BATCH, D_MODEL)),
        'cache_attn_k': nrm((L, DEC_BATCH, cache_len, N_KV, HEAD_DIM)),
        'cache_attn_v': nrm((L, DEC_BATCH, cache_len, N_KV, HEAD_DIM)),
        'state_pool': nrm((L, DEC_BATCH, POOL_HIST, POOL_WIDTH)),
        'state_rwkv_shift': nrm((L, DEC_BATCH, 1, RW_IN)),
        'state_rwkv_wkv': nrm((L, DEC_BATCH, RW_HEADS, RW_HD, RW_HD), 0.5),
        'norm1_g': 1.0 + nrm((L, D_MODEL), 0.05),
        'norm2_g': 1.0 + nrm((L, D_MODEL), 0.05),
        'ada_w': nrm((L, D_MODEL, 6 * D_MODEL), 0.5 * D_MODEL ** -0.5),
        'ada_b': nrm((L, 6 * D_MODEL), 0.02),
        'w_in': nrm((L, D_MODEL, IN_COLS), D_MODEL ** -0.5),
        'pool_w': nrm((L, len(POOL_WINDOWS), POOL_GROUP, POOL_GROUP), POOL_GROUP ** -0.5),
        'pool_scale': 1.0 + nrm((L, POOL_WIDTH), 0.1),
        'q_norm_g': 1.0 + nrm((L, HEAD_DIM), 0.05),
        'k_norm_g': 1.0 + nrm((L, HEAD_DIM), 0.05),
        'attn_sink': nrm((L, N_HEADS), 0.5),
        'rw_mu': uni((L, RW_IN), 0.0, 1.0),
        'rw_w0': uni((L, RW_WIDTH), -6.0, -1.0),
        'rw_w2': nrm((L, RW_DECAY_LORA, RW_WIDTH), 0.5 * RW_DECAY_LORA ** -0.5),
        'rw_a0': nrm((L, RW_WIDTH), 0.1),
        'rw_a2': nrm((L, RW_A_LORA, RW_WIDTH), RW_A_LORA ** -0.5),
        'rw_g2': nrm((L, RW_GATE_LORA, RW_WIDTH), RW_GATE_LORA ** -0.5),
        'rw_k_k': 0.85 + nrm((L, RW_WIDTH), 0.05),
        'rw_k_a': 1.0 + nrm((L, RW_WIDTH), 0.05),
        'rw_r_k': nrm((L, RW_HEADS, RW_HD), 0.1),
        'rw_ln_w': 1.0 + nrm((L, RW_WIDTH), 0.05),
        'rw_ln_b': nrm((L, RW_WIDTH), 0.02),
        'proj_pool': nrm((L, POOL_WIDTH, D_MODEL), POOL_WIDTH ** -0.5),
        'proj_attn': nrm((L, ATTN_WIDTH, D_MODEL), ATTN_WIDTH ** -0.5),
        'proj_rwkv': nrm((L, RW_WIDTH, D_MODEL), RW_WIDTH ** -0.5),
        'w_out': nrm((L, D_MODEL, D_MODEL), D_MODEL ** -0.5),
        'ffn_w1': nrm((N_DENSE, D_MODEL, D_FF), D_MODEL ** -0.5),
        'ffn_w3': nrm((N_DENSE, D_MODEL, D_FF), D_MODEL ** -0.5),
        'ffn_w2': nrm((N_DENSE, D_FF, D_MODEL), D_FF ** -0.5),
        'router_w': nrm((N_MOE, D_MODEL, N_EXPERTS), D_MODEL ** -0.5),
        'router_b': nrm((N_MOE, N_EXPERTS), 0.01),
        'moe_w1': nrm((N_MOE, N_EXPERTS, D_MODEL, D_FF_EXPERT), D_MODEL ** -0.5),
        'moe_w3': nrm((N_MOE, N_EXPERTS, D_MODEL, D_FF_EXPERT), D_MODEL ** -0.5),
        'moe_w2': nrm((N_MOE, N_EXPERTS, D_FF_EXPERT, D_MODEL), D_FF_EXPERT ** -0.5),
    }


def reference(x_prompt, x_sample, c_prompt, c_sample, cache_attn_k, cache_attn_v, state_pool,
              state_rwkv_shift, state_rwkv_wkv, norm1_g, norm2_g, ada_w, ada_b, w_in, pool_w,
              pool_scale, q_norm_g, k_norm_g, attn_sink, rw_mu, rw_w0, rw_w2, rw_a0, rw_a2, rw_g2,
              rw_k_k, rw_k_a, rw_r_k, rw_ln_w, rw_ln_b, proj_pool, proj_attn, proj_rwkv, w_out,
              ffn_w1, ffn_w3, ffn_w2, router_w, router_b, moe_w1, moe_w3, moe_w2):
    W = dict(norm1_g=norm1_g, norm2_g=norm2_g, ada_w=ada_w, ada_b=ada_b, w_in=w_in, pool_w=pool_w,
             pool_scale=pool_scale, q_norm_g=q_norm_g, k_norm_g=k_norm_g, attn_sink=attn_sink,
             rw_mu=rw_mu, rw_w0=rw_w0, rw_w2=rw_w2, rw_a0=rw_a0, rw_a2=rw_a2, rw_g2=rw_g2,
             rw_k_k=rw_k_k, rw_k_a=rw_k_a, rw_r_k=rw_r_k, rw_ln_w=rw_ln_w, rw_ln_b=rw_ln_b,
             proj_pool=proj_pool, proj_attn=proj_attn, proj_rwkv=proj_rwkv, w_out=w_out,
             ffn_w1=ffn_w1, ffn_w3=ffn_w3, ffn_w2=ffn_w2, router_w=router_w, router_b=router_b,
             moe_w1=moe_w1, moe_w3=moe_w3, moe_w2=moe_w2)
    y_prompt, (p_attn_k, p_attn_v, p_pool, p_shift, p_wkv) = _trunk(x_prompt, c_prompt, W, 0)
    y_sample, (s_attn_k, s_attn_v, s_pool, s_shift, s_wkv) = _trunk(
        x_sample, c_sample, W, PAST_LEN, cache_attn_k, cache_attn_v, state_pool, state_rwkv_shift, state_rwkv_wkv)
    return (y_prompt, y_sample, p_attn_k, p_attn_v, p_pool, p_shift, p_wkv,
            s_attn_k, s_attn_v, s_pool, s_shift, s_wkv)
```

```python
import functools

import jax
import jax.numpy as jnp
from jax import lax
from jax.experimental import pallas as pl
from jax.experimental.pallas import tpu as pltpu

F32 = jnp.float32
BF16 = jnp.bfloat16

D_MODEL = 1024
CHUNK = 64
NORM_EPS = 1e-6
N_BRANCH = 3
POOL_WINDOWS = (2, 4, 8, 16)
POOL_GROUP = 128
POOL_WIDTH = 512
POOL_HIST = 15
HEAD_DIM = 64
N_HEADS = 8
N_KV = 2
ATTN_WIDTH = 512
KV_WIDTH = 128
WINDOW = 128
ATTN_SCALE = HEAD_DIM ** -0.5
RW_HD = 64
RW_HEADS = 8
RW_WIDTH = 512
RW_DECAY_LORA = 64
RW_A_LORA = 64
RW_GATE_LORA = 160
RW_IN = 3 * RW_WIDTH + RW_DECAY_LORA + RW_A_LORA + RW_GATE_LORA
RW_GN_EPS = 64e-5
OFF_Q = POOL_WIDTH
OFF_K = OFF_Q + ATTN_WIDTH
OFF_V = OFF_K + KV_WIDTH
OFF_RW = OFF_V + KV_WIDTH
OFF_GATE = OFF_RW + RW_IN
N_EXPERTS = 8

LANES = 128
RW_PAD = 1920
RW_LORA_OFF = 3 * RW_WIDTH
RW_GATE_OFF = RW_LORA_OFF + LANES
GATE_COLS = N_BRANCH * D_MODEL
PAD_GATE = OFF_RW + RW_PAD
IN_COLS_PAD = PAD_GATE + GATE_COLS
HEADS_PER_PACK = 4
PACK = HEADS_PER_PACK * RW_HD
NEG_BIG = -1e30
VMEM_LIMIT = 56 * 1024 * 1024


def _cparams(sem):
    return pltpu.CompilerParams(dimension_semantics=sem, vmem_limit_bytes=VMEM_LIMIT)


def _dot(a, b):
    return jnp.dot(a.astype(BF16), b.astype(BF16), preferred_element_type=F32)


def _dot_nt(a, b):
    return lax.dot_general(a.astype(BF16), b.astype(BF16), (((1,), (1,)), ((), ())),
                           preferred_element_type=F32)


def _dot_tn(a, b):
    return lax.dot_general(a.astype(BF16), b.astype(BF16), (((0,), (0,)), ((), ())),
                           preferred_element_type=F32)


def _dot_split(x, g):
    hi = x.astype(BF16)
    lo = (x - hi.astype(F32)).astype(BF16)
    return (jnp.dot(hi, g, preferred_element_type=F32)
            + jnp.dot(lo, g, preferred_element_type=F32))


def _dot_f32(a, b):
    return jnp.dot(a, b, preferred_element_type=F32, precision=lax.Precision.HIGHEST)


def _sigmoid(x):
    return 1.0 / (1.0 + jnp.exp(-x))


def _group_ones(width, group):
    r = lax.broadcasted_iota(jnp.int32, (width, width), 0) // group
    c = lax.broadcasted_iota(jnp.int32, (width, width), 1) // group
    return (r == c).astype(BF16)


def _tile_rows(bsz, seq, rows):
    if seq >= rows:
        assert seq % rows == 0
        return 1, rows
    bb = max(1, min(bsz, rows // seq))
    while bsz % bb:
        bb -= 1
    return bb, seq


def _mod_kernel(c_ref, w_ref, b_ref, o_ref):
    c = c_ref[...]
    o_ref[0] = _dot(c * _sigmoid(c), w_ref[0]) + b_ref[0]


def _ada_mod(c_all, ada_w, ada_b):
    depth, d, cols = ada_w.shape
    rows = c_all.shape[0]
    tn = 2048
    return pl.pallas_call(
        _mod_kernel,
        grid=(depth, cols // tn),
        in_specs=[pl.BlockSpec((rows, d), lambda l, j: (0, 0)),
                  pl.BlockSpec((1, d, tn), lambda l, j: (l, 0, j)),
                  pl.BlockSpec((1, 1, tn), lambda l, j: (l, 0, j))],
        out_specs=pl.BlockSpec((1, rows, tn), lambda l, j: (l, 0, j)),
        out_shape=jax.ShapeDtypeStruct((depth, rows, cols), F32),
        compiler_params=_cparams(("parallel", "parallel")),
        name="ada_mod",
    )(c_all, ada_w, ada_b.reshape(depth, 1, cols))


def _inproj_kernel(x_ref, sh_ref, sc_ref, g_ref, w_ref, qg_ref, kg_ref,
                   u_ref, q_ref, k_ref, v_ref, p_ref, gt_ref):
    bb, tt, d = x_ref.shape
    m = bb * tt
    x = x_ref[...]
    ms = jnp.mean(x * x, axis=-1, keepdims=True)
    h = x * lax.rsqrt(ms + NORM_EPS) * g_ref[...]
    h = h * (1.0 + sc_ref[...]) + sh_ref[...]
    hb = h.astype(BF16).reshape(m, d)

    u_ref[...] = jnp.dot(hb, w_ref[:, 0:OFF_Q], preferred_element_type=F32).reshape(bb, tt, POOL_WIDTH)

    zq = jnp.dot(hb, w_ref[:, OFF_Q:OFF_K], preferred_element_type=F32)
    msq = _dot_split(zq * zq, _group_ones(ATTN_WIDTH, HEAD_DIM)) * (1.0 / HEAD_DIM)
    qn = zq * lax.rsqrt(msq + NORM_EPS) * qg_ref[...] * ATTN_SCALE
    q_ref[...] = qn.astype(BF16).reshape(bb, tt, ATTN_WIDTH)

    zk = jnp.dot(hb, w_ref[:, OFF_K:OFF_V], preferred_element_type=F32)
    msk = _dot_split(zk * zk, _group_ones(KV_WIDTH, HEAD_DIM)) * (1.0 / HEAD_DIM)
    k_ref[...] = (zk * lax.rsqrt(msk + NORM_EPS) * kg_ref[...]).reshape(bb, tt, KV_WIDTH)

    v_ref[...] = jnp.dot(hb, w_ref[:, OFF_V:OFF_RW], preferred_element_type=F32).reshape(bb, tt, KV_WIDTH)

    p_ref[...] = jnp.dot(hb, w_ref[:, OFF_RW:PAD_GATE], preferred_element_type=F32).reshape(bb, tt, RW_PAD)

    zg = jnp.dot(hb, w_ref[:, PAD_GATE:IN_COLS_PAD], preferred_element_type=F32)
    gt_ref[...] = _sigmoid(zg).astype(BF16).reshape(bb, tt, GATE_COLS)


def _inproj(x, sh, sc, g1, w_in_p, qg, kg):
    bsz, seq, d = x.shape
    bb, tt = _tile_rows(bsz, seq, 256)
    tok = lambda f: pl.BlockSpec((bb, tt, f), lambda b, i: (b, i, 0))
    mod = pl.BlockSpec((bb, 1, d), lambda b, i: (b, 0, 0))
    full = lambda a: pl.BlockSpec(a.shape, lambda b, i: (0,) * a.ndim)
    return pl.pallas_call(
        _inproj_kernel,
        grid=(bsz // bb, seq // tt),
        in_specs=[tok(d), mod, mod, full(g1), full(w_in_p), full(qg), full(kg)],
        out_specs=[tok(POOL_WIDTH), tok(ATTN_WIDTH), tok(KV_WIDTH), tok(KV_WIDTH), tok(RW_PAD), tok(GATE_COLS)],
        out_shape=[jax.ShapeDtypeStruct((bsz, seq, POOL_WIDTH), F32),
                   jax.ShapeDtypeStruct((bsz, seq, ATTN_WIDTH), BF16),
                   jax.ShapeDtypeStruct((bsz, seq, KV_WIDTH), F32),
                   jax.ShapeDtypeStruct((bsz, seq, KV_WIDTH), F32),
                   jax.ShapeDtypeStruct((bsz, seq, RW_PAD), F32),
                   jax.ShapeDtypeStruct((bsz, seq, GATE_COLS), BF16)],
        compiler_params=_cparams(("parallel", "parallel")),
        name="inproj",
    )(x, sh, sc, g1, w_in_p, qg, kg)


def _pool_kernel(u_ref, uprev_ref, hist_ref, pw_ref, ps_ref, o_ref, *, pos0):
    i = pl.program_id(1)
    tt = u_ref.shape[1]
    halo = uprev_ref.shape[1]
    u = u_ref[0]
    prev = jnp.where(i == 0, hist_ref[0], uprev_ref[0])
    ext = jnp.concatenate([prev, u], axis=0)
    sums = {}
    acc = ext
    for w in POOL_WINDOWS:
        acc = acc + pltpu.roll(acc, w // 2, axis=0)
        sums[w] = acc
    pos = pos0 + i * tt + lax.broadcasted_iota(jnp.int32, (tt, 1), 0)
    outs = []
    for g, w in enumerate(POOL_WINDOWS):
        sl = slice(g * POOL_GROUP, (g + 1) * POOL_GROUP)
        cnt = jnp.minimum(pos + 1, w).astype(F32)
        dlt = sums[w][halo:, sl] / cnt - u[:, sl]
        outs.append(_dot(dlt, pw_ref[g]))
    y = jnp.concatenate(outs, axis=1) * ps_ref[...]
    o_ref[0] = y.astype(BF16)


def _pool(u, hist16, pool_w, pool_scale, pos0):
    bsz, seq, _ = u.shape
    tt = min(seq, 512)
    halo = hist16.shape[1]
    per = tt // halo
    return pl.pallas_call(
        functools.partial(_pool_kernel, pos0=pos0),
        grid=(bsz, seq // tt),
        in_specs=[pl.BlockSpec((1, tt, POOL_WIDTH), lambda b, i: (b, i, 0)),
                  pl.BlockSpec((1, halo, POOL_WIDTH), lambda b, i: (b, jnp.maximum(i * per - 1, 0), 0)),
                  pl.BlockSpec((1, halo, POOL_WIDTH), lambda b, i: (b, 0, 0)),
                  pl.BlockSpec(pool_w.shape, lambda b, i: (0, 0, 0)),
                  pl.BlockSpec(pool_scale.shape, lambda b, i: (0, 0))],
        out_specs=pl.BlockSpec((1, tt, POOL_WIDTH), lambda b, i: (b, i, 0)),
        out_shape=jax.ShapeDtypeStruct((bsz, seq, POOL_WIDTH), BF16),
        compiler_params=_cparams(("parallel", "arbitrary")),
        name="pool",
    )(u, u, hist16, pool_w, pool_scale)


def _attn_core(q, kcat, vcat, valid, sink_ref):
    lo = lax.broadcasted_iota(jnp.int32, (1, KV_WIDTH), 1) < HEAD_DIM
    krot = pltpu.roll(kcat, HEAD_DIM, axis=1)
    vrot = pltpu.roll(vcat, HEAD_DIM, axis=1)
    pieces = []
    for g in range(N_KV):
        ksrc_lo, ksrc_hi = (kcat, krot) if g == 0 else (krot, kcat)
        vsrc_lo, vsrc_hi = (vcat, vrot) if g == 0 else (vrot, vcat)
        k_lo = jnp.where(lo, ksrc_lo, 0.0).astype(BF16)
        k_hi = jnp.where(lo, 0.0, ksrc_hi).astype(BF16)
        v_lo = jnp.where(lo, vsrc_lo, 0.0).astype(BF16)
        v_hi = jnp.where(lo, 0.0, vsrc_hi).astype(BF16)
        for jj in range(2):
            pair = 2 * g + jj
            qp = q[:, pair * LANES:(pair + 1) * LANES]
            o = None
            for half, (kh, vh) in enumerate(((k_lo, v_lo), (k_hi, v_hi))):
                sink = sink_ref[2 * pair + half]
                s = _dot_nt(qp, kh)
                s = jnp.where(valid, s, NEG_BIG)
                mx = jnp.maximum(jnp.max(s, axis=-1, keepdims=True), sink)
                pr = jnp.exp(s - mx)
                den = jnp.sum(pr, axis=-1, keepdims=True) + jnp.exp(sink - mx)
                oh = jnp.dot(pr.astype(BF16), vh, preferred_element_type=F32) / den
                o = oh if o is None else o + oh
            pieces.append(o)
    return jnp.concatenate(pieces, axis=1)


def _attn_banded_kernel(sink_ref, q_ref, k0_ref, k1_ref, k2_ref, v0_ref, v1_ref, v2_ref, o_ref):
    n = pl.program_id(1)
    kcat = jnp.concatenate([k0_ref[0], k1_ref[0], k2_ref[0]], axis=0)
    vcat = jnp.concatenate([v0_ref[0], v1_ref[0], v2_ref[0]], axis=0)
    col = lax.broadcasted_iota(jnp.int32, (1, 3 * CHUNK), 1)
    valid = col >= (2 - jnp.minimum(n, 2)) * CHUNK
    o_ref[0] = _attn_core(q_ref[0], kcat, vcat, valid, sink_ref).astype(BF16)


def _attn_banded(q, k, v, sink):
    bsz, seq, _ = q.shape
    nc = seq // CHUNK
    kv = lambda back: pl.BlockSpec((1, CHUNK, KV_WIDTH), lambda b, n: (b, jnp.maximum(n - back, 0), 0))
    return pl.pallas_call(
        _attn_banded_kernel,
        grid=(bsz, nc),
        in_specs=[pl.BlockSpec(memory_space=pltpu.SMEM),
                  pl.BlockSpec((1, CHUNK, ATTN_WIDTH), lambda b, n: (b, n, 0)),
                  kv(2), kv(1), kv(0), kv(2), kv(1), kv(0)],
        out_specs=pl.BlockSpec((1, CHUNK, ATTN_WIDTH), lambda b, n: (b, n, 0)),
        out_shape=jax.ShapeDtypeStruct((bsz, seq, ATTN_WIDTH), BF16),
        compiler_params=_cparams(("parallel", "parallel")),
        name="attn_banded",
    )(sink, q, k, k, k, v, v, v)


def _attn_cached_kernel(sink_ref, q_ref, ck_ref, k_ref, cv_ref, v_ref, o_ref):
    cache, new = ck_ref.shape[1], k_ref.shape[1]
    span = 3 * CHUNK
    pad = jnp.zeros((span - cache - new, KV_WIDTH), F32)
    kcat = jnp.concatenate([ck_ref[0], k_ref[0], pad], axis=0)
    vcat = jnp.concatenate([cv_ref[0], v_ref[0], pad], axis=0)
    valid = lax.broadcasted_iota(jnp.int32, (1, span), 1) < cache + new
    o_ref[0] = _attn_core(q_ref[0], kcat, vcat, valid, sink_ref).astype(BF16)


def _attn_cached(q, k, v, ck, cv, sink):
    bsz, seq, _ = q.shape
    cache = ck.shape[1]
    assert cache + seq <= 3 * CHUNK and (3 * CHUNK - cache - seq) % 8 == 0
    blk = lambda t, f: pl.BlockSpec((1, t, f), lambda b: (b, 0, 0))
    return pl.pallas_call(
        _attn_cached_kernel,
        grid=(bsz,),
        in_specs=[pl.BlockSpec(memory_space=pltpu.SMEM), blk(seq, ATTN_WIDTH),
                  blk(cache, KV_WIDTH), blk(seq, KV_WIDTH), blk(cache, KV_WIDTH), blk(seq, KV_WIDTH)],
        out_specs=blk(seq, ATTN_WIDTH),
        out_shape=jax.ShapeDtypeStruct((bsz, seq, ATTN_WIDTH), BF16),
        compiler_params=_cparams(("parallel",)),
        name="attn_cached",
    )(sink, q, ck, k, cv, v)


def _block_diag_rows(x, reps, mask):
    return jnp.where(mask, jnp.concatenate([x] * reps, axis=0), jnp.zeros((), x.dtype))


def _rwkv_kernel(p_ref, pprev_ref, hist_ref, s0_ref, mu_ref, vec_ref, wwa_ref, g2_ref,
                 o_ref, sout_ref, s_scr, *, chunk):
    i = pl.program_id(1)
    tt = p_ref.shape[1]
    c_len = chunk
    hp = HEADS_PER_PACK
    wide = hp * c_len

    @pl.when(i == 0)
    def _():
        s_scr[...] = s0_ref[0]

    p = p_ref[0]
    prev_row = jnp.where(i == 0, hist_ref[0], pprev_ref[0, 7:8, :])
    row = lax.broadcasted_iota(jnp.int32, (tt, 1), 0)
    prev = jnp.where(row == 0, prev_row, pltpu.roll(p, 1, axis=0))
    pm = p + (prev - p) * mu_ref[...]

    w0, a0, k_k, k_a = vec_ref[0:1], vec_ref[1:2], vec_ref[2:3], vec_ref[3:4]
    r_k, ln_w, ln_b = vec_ref[4:5], vec_ref[5:6], vec_ref[6:7]
    r = pm[:, 0:RW_WIDTH]
    k = pm[:, RW_WIDTH:2 * RW_WIDTH]
    v = pm[:, 2 * RW_WIDTH:3 * RW_WIDTH]
    xa = pm[:, RW_LORA_OFF:RW_LORA_OFF + LANES]
    lane = lax.broadcasted_iota(jnp.int32, (1, LANES), 1)
    xa = jnp.where(lane < RW_DECAY_LORA, jnp.tanh(xa), xa)
    la = _dot(xa, wwa_ref[...])
    wlin = -(w0 + la[:, :RW_WIDTH])
    softplus = jnp.maximum(wlin, 0.0) + jnp.log(1.0 + jnp.exp(-jnp.abs(wlin)))
    lw = -jnp.exp(-softplus - 0.5)
    a = _sigmoid(a0 + la[:, RW_WIDTH:])
    g = _dot(_sigmoid(pm[:, RW_GATE_OFF:RW_PAD]), g2_ref[...])
    ones_g = _group_ones(RW_WIDTH, RW_HD)
    kk = k * k_k
    kk = kk / jnp.maximum(jnp.sqrt(_dot_split(kk * kk, ones_g)), 1e-12)
    k2 = k * (1.0 + (a - 1.0) * k_a)
    bonus = _dot_split(r * k2 * r_k, ones_g) * v
    kka = kk * a

    rowc = lax.broadcasted_iota(jnp.int32, (c_len, 1), 0)
    bd_r = lax.broadcasted_iota(jnp.int32, (wide, PACK), 0) // c_len
    bd_c = lax.broadcasted_iota(jnp.int32, (wide, PACK), 1) // RW_HD
    mask_bd = bd_r == bd_c
    sq_r = lax.broadcasted_iota(jnp.int32, (wide, wide), 0) // c_len
    sq_c = lax.broadcasted_iota(jnp.int32, (wide, wide), 1) // c_len
    mask_bd2 = sq_r == sq_c
    t_idx = lax.broadcasted_iota(jnp.int32, (c_len, wide), 0)
    s_idx = lax.broadcasted_iota(jnp.int32, (c_len, wide), 1) % c_len
    strict = s_idx < t_idx
    lower = s_idx <= t_idx
    eye_p = (s_idx == t_idx).astype(F32)
    st_r = lax.broadcasted_iota(jnp.int32, (PACK, PACK), 0)
    st_c = lax.broadcasted_iota(jnp.int32, (PACK, PACK), 1)
    mask_state = (st_r // RW_HD) == (st_c // RW_HD)
    eye_state = st_r == st_c
    zeros_bd = jnp.zeros((wide, PACK), BF16)

    bd = lambda x: _block_diag_rows(x.astype(BF16), hp, mask_bd)
    bd2 = lambda x: _block_diag_rows(x.astype(BF16), hp, mask_bd2)

    ys = []
    for c in range(tt // c_len):
        sl = slice(c * c_len, (c + 1) * c_len)
        lwc = lw[sl]
        cs = lwc
        sft = 1
        while sft < c_len:
            cs = cs + jnp.where(rowc >= sft, pltpu.roll(cs, sft, axis=0), 0.0)
            sft *= 2
        tot = cs[c_len - 1:c_len, :]
        p_in = jnp.exp(cs)
        p_inv = jnp.exp(-cs)
        p_end = jnp.exp(tot - cs)
        p_tot = jnp.exp(tot)
        at = -kk[sl] * jnp.exp(cs - lwc)
        bt = kka[sl] * p_inv
        kt = k2[sl] * p_inv
        rt = r[sl] * p_in
        bb_ = kka[sl] * p_end
        kb = k2[sl] * p_end
        vc = v[sl]
        y_halves = []
        for j in range(RW_WIDTH // PACK):
            ln = slice(j * PACK, (j + 1) * PACK)
            s1 = _dot_nt(jnp.concatenate([at[:, ln], rt[:, ln]], axis=0),
                         jnp.concatenate([bd(bt[:, ln]), bd(kt[:, ln])], axis=0))
            l_ab = jnp.where(strict, s1[:c_len, :wide], 0.0)
            l_ak = jnp.where(strict, s1[:c_len, wide:], 0.0)
            g_b = jnp.where(lower, s1[c_len:, :wide], 0.0)
            g_k = jnp.where(lower, s1[c_len:, wide:], 0.0)
            pw = _dot(l_ab, bd2(l_ab))
            tm = eye_p + l_ab
            span = 2
            while span < c_len:
                rhs = bd2(pw)
                if 2 * span < c_len:
                    both = _dot(jnp.concatenate([pw, tm], axis=0), rhs)
                    pw = both[:c_len]
                    tm = tm + both[c_len:]
                else:
                    tm = tm + _dot(tm, rhs)
                span *= 2
            x1 = _dot(l_ak, bd(vc[:, ln]))
            wu = _dot(tm, jnp.concatenate([bd(at[:, ln]), bd(x1)], axis=1))
            wt, ut = wu[:, :PACK], wu[:, PACK:]
            gy = _dot(jnp.concatenate([g_b, g_k], axis=1),
                      jnp.concatenate([jnp.concatenate([bd(wt), bd(ut)], axis=1),
                                       jnp.concatenate([zeros_bd, bd(vc[:, ln])], axis=1)], axis=0))
            rh = rt[:, ln] + gy[:, :PACK]
            yh = gy[:, PACK:]
            mn = _dot_tn(jnp.concatenate([bb_[:, ln], kb[:, ln]], axis=0),
                         jnp.concatenate([jnp.concatenate([wt, ut], axis=1),
                                          jnp.concatenate([jnp.zeros((c_len, PACK), F32), vc[:, ln]], axis=1)],
                                         axis=0))
            mc = jnp.where(mask_state, mn[:, :PACK], 0.0) + jnp.where(eye_state, p_tot[:, ln], 0.0)
            nc = jnp.where(mask_state, mn[:, PACK:], 0.0)
            st = s_scr[j]
            y_halves.append(_dot_f32(rh, st) + yh)
            s_scr[j] = _dot_f32(mc, st) + nc
        ys.append(jnp.concatenate(y_halves, axis=1))
    y = ys[0] if len(ys) == 1 else jnp.concatenate(ys, axis=0)

    inv_hd = 1.0 / RW_HD
    mu_y = _dot_split(y, ones_g) * inv_hd
    dy = y - mu_y
    var = _dot_split(dy * dy, ones_g) * inv_hd
    yn = dy * lax.rsqrt(var + RW_GN_EPS) * ln_w + ln_b
    o_ref[0] = ((yn + bonus) * g).astype(BF16)

    @pl.when(i == pl.num_programs(1) - 1)
    def _():
        sout_ref[0] = s_scr[...]


def _rwkv(p, hist, s0_packed, mu, vec, wwa, g2p, chunks_per_tile):
    bsz, seq, _ = p.shape
    chunk = min(CHUNK, seq)
    tt = min(seq, chunk * chunks_per_tile)
    per = tt // 8
    npk = RW_WIDTH // PACK
    full = lambda a: pl.BlockSpec(a.shape, lambda b, i: (0,) * a.ndim)
    return pl.pallas_call(
        functools.partial(_rwkv_kernel, chunk=chunk),
        grid=(bsz, seq // tt),
        in_specs=[pl.BlockSpec((1, tt, RW_PAD), lambda b, i: (b, i, 0)),
                  pl.BlockSpec((1, 8, RW_PAD), lambda b, i: (b, jnp.maximum(i * per - 1, 0), 0)),
                  pl.BlockSpec((1, 1, RW_PAD), lambda b, i: (b, 0, 0)),
                  pl.BlockSpec((1, npk, PACK, PACK), lambda b, i: (b, 0, 0, 0)),
                  full(mu), full(vec), full(wwa), full(g2p)],
        out_specs=[pl.BlockSpec((1, tt, RW_WIDTH), lambda b, i: (b, i, 0)),
                   pl.BlockSpec((1, npk, PACK, PACK), lambda b, i: (b, 0, 0, 0))],
        out_shape=[jax.ShapeDtypeStruct((bsz, seq, RW_WIDTH), BF16),
                   jax.ShapeDtypeStruct((bsz, npk, PACK, PACK), F32)],
        scratch_shapes=[pltpu.VMEM((npk, PACK, PACK), F32)],
        compiler_params=_cparams(("parallel", "arbitrary")),
        name="rwkv",
    )(p, p, hist, s0_packed, mu, vec, wwa, g2p)


def _pack_state(wkv):
    bsz = wkv.shape[0]
    npk = RW_HEADS // HEADS_PER_PACK
    st = jnp.swapaxes(wkv, -1, -2).reshape(bsz, npk, HEADS_PER_PACK, RW_HD, RW_HD)
    eye = jnp.eye(HEADS_PER_PACK, dtype=wkv.dtype)
    return jnp.einsum('bjhkv,hg->bjhkgv', st, eye).reshape(bsz, npk, PACK, PACK)


def _unpack_state(sp):
    bsz, npk = sp.shape[:2]
    s6 = sp.reshape(bsz, npk, HEADS_PER_PACK, RW_HD, HEADS_PER_PACK, RW_HD)
    blocks = jnp.stack([s6[:, :, h, :, h, :] for h in range(HEADS_PER_PACK)], axis=2)
    return jnp.swapaxes(blocks.reshape(bsz, RW_HEADS, RW_HD, RW_HD), -1, -2)


def _merge_kernel(x_ref, op_ref, oa_ref, or_ref, gt_ref, ga_ref, sh_ref, sc_ref, g2_ref,
                  pp_ref, pa_ref, pr_ref, wo_ref, x1_ref, h2_ref):
    bb, tt, d = x_ref.shape
    m = bb * tt
    gates = gt_ref[...].reshape(m, GATE_COLS)
    merged = None
    for n, (o_ref, w_ref) in enumerate(((op_ref, pp_ref), (oa_ref, pa_ref), (or_ref, pr_ref))):
        o = o_ref[...].reshape(m, o_ref.shape[2])
        term = gates[:, n * d:(n + 1) * d].astype(F32) * jnp.dot(o, w_ref[...], preferred_element_type=F32)
        merged = term if merged is None else merged + term
    y = _dot(merged, wo_ref[...]).reshape(bb, tt, d)
    x1 = x_ref[...] + ga_ref[...] * y
    x1_ref[...] = x1
    ms = jnp.mean(x1 * x1, axis=-1, keepdims=True)
    h = x1 * lax.rsqrt(ms + NORM_EPS) * g2_ref[...]
    h2_ref[...] = (h * (1.0 + sc_ref[...]) + sh_ref[...]).astype(BF16)


def _merge(x, o_pool, o_attn, o_rw, gates, ga1, sh2, sc2, g2, pp, pa, pr, wo):
    bsz, seq, d = x.shape
    bb, tt = _tile_rows(bsz, seq, 512)
    tok = lambda f: pl.BlockSpec((bb, tt, f), lambda b, i: (b, i, 0))
    mod = pl.BlockSpec((bb, 1, d), lambda b, i: (b, 0, 0))
    full = lambda a: pl.BlockSpec(a.shape, lambda b, i: (0,) * a.ndim)
    return pl.pallas_call(
        _merge_kernel,
        grid=(bsz // bb, seq // tt),
        in_specs=[tok(d), tok(POOL_WIDTH), tok(ATTN_WIDTH), tok(RW_WIDTH), tok(GATE_COLS),
                  mod, mod, mod, full(g2), full(pp), full(pa), full(pr), full(wo)],
        out_specs=[tok(d), tok(d)],
        out_shape=[jax.ShapeDtypeStruct((bsz, seq, d), F32), jax.ShapeDtypeStruct((bsz, seq, d), BF16)],
        compiler_params=_cparams(("parallel", "parallel")),
        name="merge",
    )(x, o_pool, o_attn, o_rw, gates, ga1, sh2, sc2, g2, pp, pa, pr, wo)


def _swiglu_rows(h, w1, w3, w2):
    a = jnp.dot(h, w1, preferred_element_type=F32)
    b = jnp.dot(h, w3, preferred_element_type=F32)
    return jnp.dot((a * _sigmoid(a) * b).astype(BF16), w2, preferred_element_type=F32)


def _ffn_kernel(x_ref, h_ref, ga_ref, w1_ref, w3_ref, w2_ref, o_ref, *, n_split):
    bb, tt, d = x_ref.shape
    h = h_ref[...].reshape(bb * tt, d)
    ff = w1_ref.shape[1]
    step = ff // n_split
    f = None
    for s in range(n_split):
        sl = slice(s * step, (s + 1) * step)
        part = _swiglu_rows(h, w1_ref[:, sl], w3_ref[:, sl], w2_ref[sl, :])
        f = part if f is None else f + part
    o_ref[...] = x_ref[...] + ga_ref[...] * f.reshape(bb, tt, d)


def _ffn(x, h, ga, w1, w3, w2):
    bsz, seq, d = x.shape
    bb, tt = _tile_rows(bsz, seq, 512)
    tok = pl.BlockSpec((bb, tt, d), lambda b, i: (b, i, 0))
    mod = pl.BlockSpec((bb, 1, d), lambda b, i: (b, 0, 0))
    full = lambda a: pl.BlockSpec(a.shape, lambda b, i: (0,) * a.ndim)
    return pl.pallas_call(
        functools.partial(_ffn_kernel, n_split=2),
        grid=(bsz // bb, seq // tt),
        in_specs=[tok, tok, mod, full(w1), full(w3), full(w2)],
        out_specs=tok,
        out_shape=jax.ShapeDtypeStruct((bsz, seq, d), F32),
        compiler_params=_cparams(("parallel", "parallel")),
        name="ffn",
    )(x, h, ga, w1, w3, w2)


def _moe_kernel(x_ref, h_ref, ga_ref, rw_ref, rb_ref, w1_ref, w3_ref, w2_ref, o_ref, gate_scr):
    e = pl.program_id(2)
    bb, tt, d = x_ref.shape
    m = bb * tt
    h = h_ref[...].reshape(m, d)
    lane = lax.broadcasted_iota(jnp.int32, (m, LANES), 1)

    @pl.when(e == 0)
    def _():
        logits = _dot(h, rw_ref[...]) + rb_ref[...]
        logits = jnp.where(lane < N_EXPERTS, logits, NEG_BIG)
        mx = jnp.max(logits, axis=-1, keepdims=True)
        ex = jnp.exp(logits - mx)
        probs = ex / jnp.sum(ex, axis=-1, keepdims=True)
        p1 = jnp.max(probs, axis=-1, keepdims=True)
        i1 = jnp.min(jnp.where(probs == p1, lane, LANES), axis=-1, keepdims=True)
        rest = jnp.where(lane == i1, -1.0, probs)
        p2 = jnp.max(rest, axis=-1, keepdims=True)
        i2 = jnp.min(jnp.where(rest == p2, lane, LANES), axis=-1, keepdims=True)
        tot = p1 + p2
        gate_scr[...] = jnp.where(lane == i1, p1 / tot, 0.0) + jnp.where(lane == i2, p2 / tot, 0.0)
        o_ref[...] = x_ref[...]

    gate = jnp.sum(jnp.where(lane == e, gate_scr[...], 0.0), axis=-1, keepdims=True)
    f = _swiglu_rows(h, w1_ref[0], w3_ref[0], w2_ref[0])
    o_ref[...] += ga_ref[...] * (gate * f).reshape(bb, tt, d)


def _moe(x, h, ga, rw_p, rb_p, w1, w3, w2):
    bsz, seq, d = x.shape
    bb, tt = _tile_rows(bsz, seq, 1024)
    n_e, _, ffe = w1.shape
    tok = pl.BlockSpec((bb, tt, d), lambda b, i, e: (b, i, 0))
    mod = pl.BlockSpec((bb, 1, d), lambda b, i, e: (b, 0, 0))
    full = lambda a: pl.BlockSpec(a.shape, lambda b, i, e: (0,) * a.ndim)
    return pl.pallas_call(
        _moe_kernel,
        grid=(bsz // bb, seq // tt, n_e),
        in_specs=[tok, tok, mod, full(rw_p), full(rb_p),
                  pl.BlockSpec((1, d, ffe), lambda b, i, e: (e, 0, 0)),
                  pl.BlockSpec((1, d, ffe), lambda b, i, e: (e, 0, 0)),
                  pl.BlockSpec((1, ffe, d), lambda b, i, e: (e, 0, 0))],
        out_specs=tok,
        out_shape=jax.ShapeDtypeStruct((bsz, seq, d), F32),
        scratch_shapes=[pltpu.VMEM((bb * tt, LANES), F32)],
        compiler_params=_cparams(("parallel", "parallel", "arbitrary")),
        name="moe",
    )(x, h, ga, rw_p, rb_p, w1, w3, w2)


def _prep_weights(W):
    depth = W['w_in'].shape[0]
    P = {}
    w_in = W['w_in']
    pad = jnp.zeros((depth, D_MODEL, RW_PAD - RW_IN), w_in.dtype)
    P['w_in'] = jnp.concatenate([w_in[:, :, :OFF_GATE], pad, w_in[:, :, OFF_GATE:]], axis=-1).astype(BF16)
    P['qg'] = jnp.tile(W['q_norm_g'], (1, N_HEADS)).reshape(depth, 1, ATTN_WIDTH)
    P['kg'] = jnp.tile(W['k_norm_g'], (1, N_KV)).reshape(depth, 1, KV_WIDTH)
    P['g1'] = W['norm1_g'].reshape(depth, 1, D_MODEL)
    P['g2'] = W['norm2_g'].reshape(depth, 1, D_MODEL)
    P['pool_w'] = W['pool_w'].astype(BF16)
    P['pool_scale'] = W['pool_scale'].reshape(depth, 1, POOL_WIDTH)
    P['mu'] = jnp.pad(W['rw_mu'], ((0, 0), (0, RW_PAD - RW_IN))).reshape(depth, 1, RW_PAD)
    vec = jnp.stack([W['rw_w0'], W['rw_a0'], W['rw_k_k'], W['rw_k_a'],
                     W['rw_r_k'].reshape(depth, RW_WIDTH), W['rw_ln_w'], W['rw_ln_b'],
                     jnp.zeros_like(W['rw_w0'])], axis=1)
    P['vec'] = vec
    zl = jnp.zeros_like(W['rw_w2'])
    P['wwa'] = jnp.concatenate([jnp.concatenate([W['rw_w2'], zl], axis=2),
                                jnp.concatenate([zl, W['rw_a2']], axis=2)], axis=1).astype(BF16)
    P['g2p'] = jnp.pad(W['rw_g2'], ((0, 0), (0, RW_PAD - RW_GATE_OFF - RW_GATE_LORA), (0, 0))).astype(BF16)
    for name in ('proj_pool', 'proj_attn', 'proj_rwkv', 'w_out', 'ffn_w1', 'ffn_w3', 'ffn_w2',
                 'moe_w1', 'moe_w3', 'moe_w2'):
        P[name] = W[name].astype(BF16)
    P['router_w'] = jnp.pad(W['router_w'], ((0, 0), (0, 0), (0, LANES - N_EXPERTS))).astype(BF16)
    P['router_b'] = jnp.pad(W['router_b'], ((0, 0), (0, LANES - N_EXPERTS))).reshape(-1, 1, LANES)
    P['sink'] = W['attn_sink']
    return P


def _layer(x, mods, l, P, pos0, hist_pool, hist_shift, wkv0, ck, cv, chunks_per_tile):
    bsz, seq, _ = x.shape
    sh1, sc1, ga1, sh2, sc2, ga2 = mods
    u, q, k, v, p, gates = _inproj(x, sh1, sc1, P['g1'][l], P['w_in'][l], P['qg'][l], P['kg'][l])
    hist16 = jnp.pad(hist_pool, ((0, 0), (1, 0), (0, 0)))
    o_pool = _pool(u, hist16, P['pool_w'][l], P['pool_scale'][l], pos0)
    if ck is None:
        o_attn = _attn_banded(q, k, v, P['sink'][l])
        new_k, new_v = k[:, -WINDOW:], v[:, -WINDOW:]
    else:
        o_attn = _attn_cached(q, k, v, ck.reshape(bsz, -1, KV_WIDTH), cv.reshape(bsz, -1, KV_WIDTH), P['sink'][l])
        new_k, new_v = k, v
    hist_p = jnp.pad(hist_shift, ((0, 0), (0, 0), (0, RW_PAD - RW_IN)))
    o_rw, s_out = _rwkv(p, hist_p, _pack_state(wkv0), P['mu'][l], P['vec'][l], P['wwa'][l], P['g2p'][l],
                        chunks_per_tile)
    x1, h2 = _merge(x, o_pool, o_attn, o_rw, gates, ga1, sh2, sc2, P['g2'][l],
                    P['proj_pool'][l], P['proj_attn'][l], P['proj_rwkv'][l], P['w_out'][l])
    i = l // 2
    if l % 2 == 0:
        x2 = _ffn(x1, h2, ga2, P['ffn_w1'][i], P['ffn_w3'][i], P['ffn_w2'][i])
    else:
        x2 = _moe(x1, h2, ga2, P['router_w'][i], P['router_b'][i], P['moe_w1'][i], P['moe_w3'][i], P['moe_w2'][i])
    new_pool = jnp.concatenate([hist_pool, u], axis=1)[:, -POOL_HIST:]
    new = (new_k.reshape(bsz, -1, N_KV, HEAD_DIM), new_v.reshape(bsz, -1, N_KV, HEAD_DIM),
           new_pool, p[:, -1:, :RW_IN], _unpack_state(s_out))
    return x2, new


def _trunk(x, mod_all, P, pos0, states, chunks_per_tile):
    bsz = x.shape[0]
    depth = P['w_in'].shape[0]
    outs = ([], [], [], [], [])
    for l in range(depth):
        mods = tuple(mod_all[l, :, n * D_MODEL:(n + 1) * D_MODEL].reshape(bsz, 1, D_MODEL) for n in range(6))
        if states is None:
            hp = jnp.zeros((bsz, POOL_HIST, POOL_WIDTH), F32)
            hs = jnp.zeros((bsz, 1, RW_IN), F32)
            s0 = jnp.zeros((bsz, RW_HEADS, RW_HD, RW_HD), F32)
            ck = cv = None
        else:
            ck, cv, hp, hs, s0 = (s[l] for s in states)
        x, new = _layer(x, mods, l, P, pos0, hp, hs, s0, ck, cv, chunks_per_tile)
        for lst, arr in zip(outs, new):
            lst.append(arr)
    return x, [jnp.stack(lst) for lst in outs]


def kernel(x_prompt, x_sample, c_prompt, c_sample, cache_attn_k, cache_attn_v, state_pool, state_rwkv_shift, state_rwkv_wkv, norm1_g, norm2_g, ada_w, ada_b, w_in, pool_w, pool_scale, q_norm_g, k_norm_g, attn_sink, rw_mu, rw_w0, rw_w2, rw_a0, rw_a2, rw_g2, rw_k_k, rw_k_a, rw_r_k, rw_ln_w, rw_ln_b, proj_pool, proj_attn, proj_rwkv, w_out, ffn_w1, ffn_w3, ffn_w2, router_w, router_b, moe_w1, moe_w3, moe_w2):
    W = dict(norm1_g=norm1_g, norm2_g=norm2_g, w_in=w_in, pool_w=pool_w,
             pool_scale=pool_scale, q_norm_g=q_norm_g, k_norm_g=k_norm_g, attn_sink=attn_sink,
             rw_mu=rw_mu, rw_w0=rw_w0, rw_w2=rw_w2, rw_a0=rw_a0, rw_a2=rw_a2, rw_g2=rw_g2,
             rw_k_k=rw_k_k, rw_k_a=rw_k_a, rw_r_k=rw_r_k, rw_ln_w=rw_ln_w, rw_ln_b=rw_ln_b,
             proj_pool=proj_pool, proj_attn=proj_attn, proj_rwkv=proj_rwkv, w_out=w_out,
             ffn_w1=ffn_w1, ffn_w3=ffn_w3, ffn_w2=ffn_w2, router_w=router_w, router_b=router_b,
             moe_w1=moe_w1, moe_w3=moe_w3, moe_w2=moe_w2)
    P = _prep_weights(W)
    n_prompt = c_prompt.shape[0]
    mod_all = _ada_mod(jnp.concatenate([c_prompt, c_sample], axis=0), ada_w, ada_b)
    past_len = 4096
    y_prompt, p_new = _trunk(x_prompt, mod_all[:, :n_prompt], P, 0, None, 1)
    y_sample, s_new = _trunk(x_sample, mod_all[:, n_prompt:], P, past_len,
                             (cache_attn_k, cache_attn_v, state_pool, state_rwkv_shift, state_rwkv_wkv), 1)
    return (y_prompt, y_sample, *p_new, *s_new)
```

```python
import functools

import jax
import jax.numpy as jnp
from jax import lax
from jax.experimental import pallas as pl
from jax.experimental.pallas import tpu as pltpu

F32 = jnp.float32
BF16 = jnp.bfloat16

D_MODEL = 1024
CHUNK = 64
NORM_EPS = 1e-6
N_BRANCH = 3
POOL_WINDOWS = (2, 4, 8, 16)
POOL_GROUP = 128
POOL_WIDTH = 512
POOL_HIST = 15
HEAD_DIM = 64
N_HEADS = 8
N_KV = 2
ATTN_WIDTH = 512
KV_WIDTH = 128
WINDOW = 128
ATTN_SCALE = HEAD_DIM ** -0.5
RW_HD = 64
RW_HEADS = 8
RW_WIDTH = 512
RW_DECAY_LORA = 64
RW_A_LORA = 64
RW_GATE_LORA = 160
RW_IN = 3 * RW_WIDTH + RW_DECAY_LORA + RW_A_LORA + RW_GATE_LORA
RW_GN_EPS = 64e-5
OFF_Q = POOL_WIDTH
OFF_K = OFF_Q + ATTN_WIDTH
OFF_V = OFF_K + KV_WIDTH
OFF_RW = OFF_V + KV_WIDTH
OFF_GATE = OFF_RW + RW_IN
N_EXPERTS = 8
PAST_LEN = 4096

LANES = 128
RW_PAD = 1920
RW_LORA_OFF = 3 * RW_WIDTH
RW_GATE_OFF = RW_LORA_OFF + LANES
GATE_COLS = N_BRANCH * D_MODEL
PAD_GATE = OFF_RW + RW_PAD
IN_COLS_PAD = PAD_GATE + GATE_COLS
HEADS_PER_PACK = 4
PACK = HEADS_PER_PACK * RW_HD
RWKV_TILE_CHUNKS = 4
ATTN_TILE_CHUNKS = 4
NEG_BIG = -1e30
VMEM_LIMIT = 56 * 1024 * 1024


def _cparams(sem):
    return pltpu.CompilerParams(dimension_semantics=sem, vmem_limit_bytes=VMEM_LIMIT)


def _dot(a, b):
    return jnp.dot(a.astype(BF16), b.astype(BF16), preferred_element_type=F32)


def _dot_nt(a, b):
    return lax.dot_general(a.astype(BF16), b.astype(BF16), (((1,), (1,)), ((), ())),
                           preferred_element_type=F32)


def _dot_tn(a, b):
    return lax.dot_general(a.astype(BF16), b.astype(BF16), (((0,), (0,)), ((), ())),
                           preferred_element_type=F32)


def _split_bf16(x):
    hi = x.astype(BF16)
    return hi, (x - hi.astype(F32)).astype(BF16)


def _sigmoid(x):
    return 1.0 / (1.0 + jnp.exp(-x))


def _group_ones(width, group):
    r = lax.broadcasted_iota(jnp.int32, (width, width), 0) // group
    c = lax.broadcasted_iota(jnp.int32, (width, width), 1) // group
    return (r == c).astype(BF16)


def _tile_rows(bsz, seq, rows):
    if seq >= rows:
        assert seq % rows == 0
        return 1, rows
    bb = max(1, min(bsz, rows // seq))
    while bsz % bb:
        bb -= 1
    return bb, seq


def _mod_kernel(c_ref, w_ref, b_ref, o_ref):
    c = c_ref[...]
    o_ref[0] = _dot(c * _sigmoid(c), w_ref[0]) + b_ref[0]


def _ada_mod(c_all, ada_w, ada_b):
    depth, d, cols = ada_w.shape
    rows = c_all.shape[0]
    tn = 2048
    return pl.pallas_call(
        _mod_kernel,
        grid=(depth, cols // tn),
        in_specs=[pl.BlockSpec((rows, d), lambda l, j: (0, 0)),
                  pl.BlockSpec((1, d, tn), lambda l, j: (l, 0, j)),
                  pl.BlockSpec((1, 1, tn), lambda l, j: (l, 0, j))],
        out_specs=pl.BlockSpec((1, rows, tn), lambda l, j: (l, 0, j)),
        out_shape=jax.ShapeDtypeStruct((depth, rows, cols), F32),
        compiler_params=_cparams(("parallel", "parallel")),
        name="ada_mod",
    )(c_all, ada_w, ada_b.reshape(depth, 1, cols))


def _inproj_kernel(x_ref, sh_ref, sc_ref, g_ref, w_ref, qg_ref, kg_ref,
                   u_ref, q_ref, k_ref, v_ref, p_ref, gt_ref):
    bb, tt, d = x_ref.shape
    m = bb * tt
    x = x_ref[...]
    ms = jnp.mean(x * x, axis=-1, keepdims=True)
    h = x * lax.rsqrt(ms + NORM_EPS) * g_ref[...]
    h = h * (1.0 + sc_ref[...]) + sh_ref[...]
    hb = h.astype(BF16).reshape(m, d)

    u_ref[...] = jnp.dot(hb, w_ref[:, 0:OFF_Q], preferred_element_type=F32).reshape(bb, tt, POOL_WIDTH)

    zq = jnp.dot(hb, w_ref[:, OFF_Q:OFF_K], preferred_element_type=F32)
    msq = _dot(zq * zq, _group_ones(ATTN_WIDTH, HEAD_DIM)) * (1.0 / HEAD_DIM)
    qn = zq * lax.rsqrt(msq + NORM_EPS) * qg_ref[...] * ATTN_SCALE
    q_ref[...] = qn.astype(BF16).reshape(bb, tt, ATTN_WIDTH)

    zk = jnp.dot(hb, w_ref[:, OFF_K:OFF_V], preferred_element_type=F32)
    msk = _dot(zk * zk, _group_ones(KV_WIDTH, HEAD_DIM)) * (1.0 / HEAD_DIM)
    k_ref[...] = (zk * lax.rsqrt(msk + NORM_EPS) * kg_ref[...]).reshape(bb, tt, KV_WIDTH)

    v_ref[...] = jnp.dot(hb, w_ref[:, OFF_V:OFF_RW], preferred_element_type=F32).reshape(bb, tt, KV_WIDTH)

    p_ref[...] = jnp.dot(hb, w_ref[:, OFF_RW:PAD_GATE], preferred_element_type=F32).reshape(bb, tt, RW_PAD)

    zg = jnp.dot(hb, w_ref[:, PAD_GATE:IN_COLS_PAD], preferred_element_type=F32)
    gt_ref[...] = _sigmoid(zg).astype(BF16).reshape(bb, tt, GATE_COLS)


def _inproj(x, sh, sc, g1, w_in_p, qg, kg):
    bsz, seq, d = x.shape
    bb, tt = _tile_rows(bsz, seq, 256)
    tok = lambda f: pl.BlockSpec((bb, tt, f), lambda b, i: (b, i, 0))
    mod = pl.BlockSpec((bb, 1, d), lambda b, i: (b, 0, 0))
    full = lambda a: pl.BlockSpec(a.shape, lambda b, i: (0,) * a.ndim)
    return pl.pallas_call(
        _inproj_kernel,
        grid=(bsz // bb, seq // tt),
        in_specs=[tok(d), mod, mod, full(g1), full(w_in_p), full(qg), full(kg)],
        out_specs=[tok(POOL_WIDTH), tok(ATTN_WIDTH), tok(KV_WIDTH), tok(KV_WIDTH), tok(RW_PAD), tok(GATE_COLS)],
        out_shape=[jax.ShapeDtypeStruct((bsz, seq, POOL_WIDTH), F32),
                   jax.ShapeDtypeStruct((bsz, seq, ATTN_WIDTH), BF16),
                   jax.ShapeDtypeStruct((bsz, seq, KV_WIDTH), F32),
                   jax.ShapeDtypeStruct((bsz, seq, KV_WIDTH), F32),
                   jax.ShapeDtypeStruct((bsz, seq, RW_PAD), F32),
                   jax.ShapeDtypeStruct((bsz, seq, GATE_COLS), BF16)],
        compiler_params=_cparams(("parallel", "parallel")),
        name="inproj",
    )(x, sh, sc, g1, w_in_p, qg, kg)


def _pool_kernel(u_ref, uprev_ref, hist_ref, pw_ref, ps_ref, o_ref, *, pos0):
    i = pl.program_id(1)
    tt = u_ref.shape[1]
    halo = uprev_ref.shape[1]
    u = u_ref[0]
    prev = jnp.where(i == 0, hist_ref[0], uprev_ref[0])
    ext = jnp.concatenate([prev, u], axis=0)
    sums = {}
    acc = ext
    for w in POOL_WINDOWS:
        acc = acc + pltpu.roll(acc, w // 2, axis=0)
        sums[w] = acc
    pos = pos0 + i * tt + lax.broadcasted_iota(jnp.int32, (tt, 1), 0)
    outs = []
    for g, w in enumerate(POOL_WINDOWS):
        sl = slice(g * POOL_GROUP, (g + 1) * POOL_GROUP)
        cnt = jnp.minimum(pos + 1, w).astype(F32)
        dlt = sums[w][halo:, sl] / cnt - u[:, sl]
        outs.append(_dot(dlt, pw_ref[g]))
    y = jnp.concatenate(outs, axis=1) * ps_ref[...]
    o_ref[0] = y.astype(BF16)


def _pool(u, hist16, pool_w, pool_scale, pos0):
    bsz, seq, _ = u.shape
    tt = min(seq, 512)
    halo = hist16.shape[1]
    per = tt // halo
    return pl.pallas_call(
        functools.partial(_pool_kernel, pos0=pos0),
        grid=(bsz, seq // tt),
        in_specs=[pl.BlockSpec((1, tt, POOL_WIDTH), lambda b, i: (b, i, 0)),
                  pl.BlockSpec((1, halo, POOL_WIDTH), lambda b, i: (b, jnp.maximum(i * per - 1, 0), 0)),
                  pl.BlockSpec((1, halo, POOL_WIDTH), lambda b, i: (b, 0, 0)),
                  pl.BlockSpec(pool_w.shape, lambda b, i: (0, 0, 0)),
                  pl.BlockSpec(pool_scale.shape, lambda b, i: (0, 0))],
        out_specs=pl.BlockSpec((1, tt, POOL_WIDTH), lambda b, i: (b, i, 0)),
        out_shape=jax.ShapeDtypeStruct((bsz, seq, POOL_WIDTH), BF16),
        compiler_params=_cparams(("parallel", "arbitrary")),
        name="pool",
    )(u, u, hist16, pool_w, pool_scale)


def _attn_core(q, kcat, vcat, bias, sink_ref):
    lo = lax.broadcasted_iota(jnp.int32, (1, KV_WIDTH), 1) < HEAD_DIM
    krot = pltpu.roll(kcat, HEAD_DIM, axis=1)
    vrot = pltpu.roll(vcat, HEAD_DIM, axis=1)
    khs, vhs, qps, sinks = [], [], [], []
    for g in range(N_KV):
        ksrc_lo, ksrc_hi = (kcat, krot) if g == 0 else (krot, kcat)
        vsrc_lo, vsrc_hi = (vcat, vrot) if g == 0 else (vrot, vcat)
        k_lo = jnp.where(lo, ksrc_lo, 0.0).astype(BF16)
        k_hi = jnp.where(lo, 0.0, ksrc_hi).astype(BF16)
        v_lo = jnp.where(lo, vsrc_lo, 0.0).astype(BF16)
        v_hi = jnp.where(lo, 0.0, vsrc_hi).astype(BF16)
        for jj in range(2):
            pair = 2 * g + jj
            qp = q[:, pair * LANES:(pair + 1) * LANES]
            for half, (kh, vh) in enumerate(((k_lo, v_lo), (k_hi, v_hi))):
                khs.append(kh)
                vhs.append(vh)
                qps.append(qp)
                sinks.append(sink_ref[2 * pair + half])
    ss = [_dot_nt(qp, kh) + bias for qp, kh in zip(qps, khs)]
    mxs = [jnp.maximum(jnp.max(s, axis=-1, keepdims=True), sk) for s, sk in zip(ss, sinks)]
    prs = [jnp.exp(s - mx) for s, mx in zip(ss, mxs)]
    inv = [1.0 / (jnp.sum(pr, axis=-1, keepdims=True) + jnp.exp(sk - mx))
           for pr, sk, mx in zip(prs, sinks, mxs)]
    ohs = [jnp.dot(pr.astype(BF16), vh, preferred_element_type=F32) * iv
           for pr, vh, iv in zip(prs, vhs, inv)]
    return jnp.concatenate([ohs[2 * p] + ohs[2 * p + 1] for p in range(N_HEADS // 2)], axis=1)


def _attn_banded_kernel(sink_ref, q_ref, kp_ref, kc_ref, vp_ref, vc_ref, band_ref, o_ref):
    i = pl.program_id(1)
    kcat = jnp.concatenate([kp_ref[0], kc_ref[0]], axis=0)
    vcat = jnp.concatenate([vp_ref[0], vc_ref[0]], axis=0)
    col = lax.broadcasted_iota(jnp.int32, (1, kcat.shape[0]), 1)
    first = jnp.where((col < WINDOW) & (i == 0), NEG_BIG, 0.0)
    o_ref[0] = _attn_core(q_ref[0], kcat, vcat, band_ref[...] + first, sink_ref).astype(BF16)


def _band_bias(tq):
    qc = jnp.arange(tq)[:, None] // CHUNK
    kc = jnp.arange(WINDOW + tq)[None, :] // CHUNK
    win = WINDOW // CHUNK
    return jnp.where((kc >= qc) & (kc <= qc + win), 0.0, NEG_BIG).astype(F32)


def _attn_banded(q, k, v, sink):
    bsz, seq, _ = q.shape
    tq = min(seq, ATTN_TILE_CHUNKS * CHUNK)
    assert seq % tq == 0 and tq % WINDOW == 0
    per = tq // WINDOW
    band = _band_bias(tq)
    prev = pl.BlockSpec((1, WINDOW, KV_WIDTH), lambda b, i: (b, jnp.maximum(i * per - 1, 0), 0))
    cur = pl.BlockSpec((1, tq, KV_WIDTH), lambda b, i: (b, i, 0))
    return pl.pallas_call(
        _attn_banded_kernel,
        grid=(bsz, seq // tq),
        in_specs=[pl.BlockSpec(memory_space=pltpu.SMEM),
                  pl.BlockSpec((1, tq, ATTN_WIDTH), lambda b, i: (b, i, 0)),
                  prev, cur, prev, cur,
                  pl.BlockSpec(band.shape, lambda b, i: (0, 0))],
        out_specs=pl.BlockSpec((1, tq, ATTN_WIDTH), lambda b, i: (b, i, 0)),
        out_shape=jax.ShapeDtypeStruct((bsz, seq, ATTN_WIDTH), BF16),
        compiler_params=_cparams(("parallel", "parallel")),
        name="attn_banded",
    )(sink, q, k, k, v, v, band)


def _attn_cached_kernel(sink_ref, q_ref, ck_ref, k_ref, cv_ref, v_ref, o_ref):
    cache, new = ck_ref.shape[1], k_ref.shape[1]
    span = 3 * CHUNK
    pad = jnp.zeros((span - cache - new, KV_WIDTH), F32)
    kcat = jnp.concatenate([ck_ref[0], k_ref[0], pad], axis=0)
    vcat = jnp.concatenate([cv_ref[0], v_ref[0], pad], axis=0)
    col = lax.broadcasted_iota(jnp.int32, (1, span), 1)
    bias = jnp.where(col < cache + new, 0.0, NEG_BIG)
    o_ref[0] = _attn_core(q_ref[0], kcat, vcat, bias, sink_ref).astype(BF16)


def _attn_cached(q, k, v, ck, cv, sink):
    bsz, seq, _ = q.shape
    cache = ck.shape[1]
    assert cache + seq <= 3 * CHUNK and (3 * CHUNK - cache - seq) % 8 == 0
    blk = lambda t, f: pl.BlockSpec((1, t, f), lambda b: (b, 0, 0))
    return pl.pallas_call(
        _attn_cached_kernel,
        grid=(bsz,),
        in_specs=[pl.BlockSpec(memory_space=pltpu.SMEM), blk(seq, ATTN_WIDTH),
                  blk(cache, KV_WIDTH), blk(seq, KV_WIDTH), blk(cache, KV_WIDTH), blk(seq, KV_WIDTH)],
        out_specs=blk(seq, ATTN_WIDTH),
        out_shape=jax.ShapeDtypeStruct((bsz, seq, ATTN_WIDTH), BF16),
        compiler_params=_cparams(("parallel",)),
        name="attn_cached",
    )(sink, q, ck, k, cv, v)


def _rwkv_kernel(p_ref, pprev_ref, hist_ref, s0_ref, mu_ref, vec_ref, wwa_ref, g2_ref,
                 ones_ref, mbd_ref, mbd2_ref, tri_ref, mst_ref,
                 o_ref, sout_ref, s_scr, *, chunk):
    i = pl.program_id(1)
    tt = p_ref.shape[1]
    c_len = chunk
    n_chunks = tt // c_len
    hp = HEADS_PER_PACK
    wide = hp * c_len
    n_pack = RW_WIDTH // PACK

    @pl.when(i == 0)
    def _():
        s_scr[...] = s0_ref[0]

    p = p_ref[0]
    prev_row = jnp.where(i == 0, hist_ref[0], pprev_ref[0, 7:8, :])
    row = lax.broadcasted_iota(jnp.int32, (tt, 1), 0)
    prev = jnp.where(row == 0, prev_row, pltpu.roll(p, 1, axis=0))
    pm = p + (prev - p) * mu_ref[...]

    w0, a0, k_k, k_a = vec_ref[0:1], vec_ref[1:2], vec_ref[2:3], vec_ref[3:4]
    r_k, ln_w, ln_b = vec_ref[4:5], vec_ref[5:6], vec_ref[6:7]
    r = pm[:, 0:RW_WIDTH]
    k = pm[:, RW_WIDTH:2 * RW_WIDTH]
    v = pm[:, 2 * RW_WIDTH:3 * RW_WIDTH]
    xa = pm[:, RW_LORA_OFF:RW_LORA_OFF + LANES]
    lane = lax.broadcasted_iota(jnp.int32, (1, LANES), 1)
    xa = jnp.where(lane < RW_DECAY_LORA, jnp.tanh(xa), xa)
    la = _dot(xa, wwa_ref[...])
    wlin = -(w0 + la[:, :RW_WIDTH])
    softplus = jnp.maximum(wlin, 0.0) + jnp.log(1.0 + jnp.exp(-jnp.abs(wlin)))
    lw = -jnp.exp(-softplus - 0.5)
    a = _sigmoid(a0 + la[:, RW_WIDTH:])
    g = _dot(_sigmoid(pm[:, RW_GATE_OFF:RW_PAD]), g2_ref[...])
    ones_g = ones_ref[...]
    kk = k * k_k
    kk = kk / jnp.maximum(jnp.sqrt(_dot(kk * kk, ones_g)), 1e-12)
    k2 = k * (1.0 + (a - 1.0) * k_a)
    bonus = _dot(r * k2 * r_k, ones_g) * v
    kka = kk * a

    in_chunk = row % c_len
    cs = lw
    sft = 1
    while sft < c_len:
        cs = cs + jnp.where(in_chunk >= sft, pltpu.roll(cs, sft, axis=0), 0.0)
        sft *= 2
    tot = jnp.concatenate(
        [jnp.broadcast_to(cs[(c + 1) * c_len - 1:(c + 1) * c_len, :], (c_len, RW_WIDTH)) for c in range(n_chunks)],
        axis=0)
    p_inv = jnp.exp(-cs)
    p_end = jnp.exp(tot - cs)
    at = -kk * jnp.exp(cs - lw)
    bt = kka * p_inv
    kt = k2 * p_inv
    rt = r * jnp.exp(cs)
    bb_ = kka * p_end
    kb = k2 * p_end

    mbd, mbd2 = mbd_ref[...], mbd2_ref[...]
    strict, lower, eye_p = tri_ref[0], tri_ref[1], tri_ref[2]
    mask_state, eye_state = mst_ref[0], mst_ref[1]
    bd = lambda x: jnp.concatenate([x.astype(BF16)] * hp, axis=0) * mbd
    bd2 = lambda x: jnp.concatenate([x.astype(BF16)] * hp, axis=0) * mbd2
    zeros_bd = jnp.zeros((wide, PACK), BF16)
    zeros_cp = jnp.zeros((c_len, PACK), F32)

    units = [(c, j) for c in range(n_chunks) for j in range(n_pack)]
    view = lambda x, u: x[u[0] * c_len:(u[0] + 1) * c_len, u[1] * PACK:(u[1] + 1) * PACK]

    s1 = [_dot_nt(jnp.concatenate([view(at, u), view(rt, u)], axis=0),
                  jnp.concatenate([bd(view(bt, u)), bd(view(kt, u))], axis=0)) for u in units]
    l_ab = [s[:c_len, :wide] * strict for s in s1]
    l_ak = [s[:c_len, wide:] * strict for s in s1]
    g_bk = [jnp.concatenate([s[c_len:, :wide] * lower, s[c_len:, wide:] * lower], axis=1) for s in s1]
    pw = [_dot(l, bd2(l)) for l in l_ab]
    tm = [eye_p + l for l in l_ab]
    span = 2
    while span < c_len:
        rhs = [bd2(x) for x in pw]
        if 2 * span < c_len:
            both = [_dot(jnp.concatenate([x, t], axis=0), rr) for x, t, rr in zip(pw, tm, rhs)]
            pw = [bo[:c_len] for bo in both]
            tm = [t + bo[c_len:] for t, bo in zip(tm, both)]
        else:
            tm = [t + _dot(t, rr) for t, rr in zip(tm, rhs)]
        span *= 2
    bdv = [bd(view(v, u)) for u in units]
    x1 = [_dot(l, bv) for l, bv in zip(l_ak, bdv)]
    wu = [_dot(t, jnp.concatenate([bd(view(at, u)), bd(x)], axis=1)) for t, u, x in zip(tm, units, x1)]
    gy = [_dot(gg, jnp.concatenate([jnp.concatenate([bd(w[:, :PACK]), bd(w[:, PACK:])], axis=1),
                                    jnp.concatenate([zeros_bd, bv], axis=1)], axis=0))
          for gg, w, bv in zip(g_bk, wu, bdv)]
    mn = [_dot_tn(jnp.concatenate([view(bb_, u), view(kb, u)], axis=0),
                  jnp.concatenate([w, jnp.concatenate([zeros_cp, view(v, u)], axis=1)], axis=0))
          for u, w in zip(units, wu)]

    y_rows = []
    for c in range(n_chunks):
        y_halves = []
        for j in range(n_pack):
            n = c * n_pack + j
            u = units[n]
            p_tot = jnp.exp(tot[c * c_len:c * c_len + 1, j * PACK:(j + 1) * PACK])
            rh = view(rt, u) + gy[n][:, :PACK]
            mc = mn[n][:, :PACK] * mask_state + eye_state * p_tot
            lhs_hi, lhs_lo = _split_bf16(jnp.concatenate([rh, mc], axis=0))
            st_hi, st_lo = _split_bf16(s_scr[j])
            rows = c_len + PACK
            main = jnp.dot(jnp.concatenate([lhs_hi, lhs_lo], axis=0), st_hi, preferred_element_type=F32)
            out = main[:rows] + main[rows:] + jnp.dot(lhs_hi, st_lo, preferred_element_type=F32)
            y_halves.append(out[:c_len] + gy[n][:, PACK:])
            s_scr[j] = out[c_len:] + mn[n][:, PACK:] * mask_state
        y_rows.append(jnp.concatenate(y_halves, axis=1))
    y = y_rows[0] if n_chunks == 1 else jnp.concatenate(y_rows, axis=0)

    inv_hd = 1.0 / RW_HD
    dy = y - _dot(y, ones_g) * inv_hd
    var = _dot(dy * dy, ones_g) * inv_hd
    yn = dy * lax.rsqrt(var + RW_GN_EPS) * ln_w + ln_b
    o_ref[0] = ((yn + bonus) * g).astype(BF16)

    @pl.when(i == pl.num_programs(1) - 1)
    def _():
        sout_ref[0] = s_scr[...]


def _rwkv_masks(c_len):
    wide = HEADS_PER_PACK * c_len
    r_blk = jnp.arange(wide)[:, None] // c_len
    mbd = (r_blk == jnp.arange(PACK)[None, :] // RW_HD).astype(BF16)
    mbd2 = (r_blk == jnp.arange(wide)[None, :] // c_len).astype(BF16)
    t_idx = jnp.arange(c_len)[:, None]
    s_idx = jnp.arange(wide)[None, :] % c_len
    tri = jnp.stack([s_idx < t_idx, s_idx <= t_idx, s_idx == t_idx]).astype(F32)
    st = jnp.arange(PACK)
    mst = jnp.stack([(st[:, None] // RW_HD) == (st[None, :] // RW_HD), st[:, None] == st[None, :]]).astype(F32)
    g_idx = jnp.arange(RW_WIDTH) // RW_HD
    ones_g = (g_idx[:, None] == g_idx[None, :]).astype(BF16)
    return ones_g, mbd, mbd2, tri, mst


def _rwkv(p, hist, s0_packed, mu, vec, wwa, g2p, chunks_per_tile):
    bsz, seq, _ = p.shape
    chunk = min(CHUNK, seq)
    tt = min(seq, chunk * chunks_per_tile)
    per = tt // 8
    npk = RW_WIDTH // PACK
    consts = _rwkv_masks(chunk)
    full = lambda a: pl.BlockSpec(a.shape, lambda b, i: (0,) * a.ndim)
    return pl.pallas_call(
        functools.partial(_rwkv_kernel, chunk=chunk),
        grid=(bsz, seq // tt),
        in_specs=[pl.BlockSpec((1, tt, RW_PAD), lambda b, i: (b, i, 0)),
                  pl.BlockSpec((1, 8, RW_PAD), lambda b, i: (b, jnp.maximum(i * per - 1, 0), 0)),
                  pl.BlockSpec((1, 1, RW_PAD), lambda b, i: (b, 0, 0)),
                  pl.BlockSpec((1, npk, PACK, PACK), lambda b, i: (b, 0, 0, 0)),
                  full(mu), full(vec), full(wwa), full(g2p)] + [full(c) for c in consts],
        out_specs=[pl.BlockSpec((1, tt, RW_WIDTH), lambda b, i: (b, i, 0)),
                   pl.BlockSpec((1, npk, PACK, PACK), lambda b, i: (b, 0, 0, 0))],
        out_shape=[jax.ShapeDtypeStruct((bsz, seq, RW_WIDTH), BF16),
                   jax.ShapeDtypeStruct((bsz, npk, PACK, PACK), F32)],
        scratch_shapes=[pltpu.VMEM((npk, PACK, PACK), F32)],
        compiler_params=_cparams(("parallel", "arbitrary")),
        name="rwkv",
    )(p, p, hist, s0_packed, mu, vec, wwa, g2p, *consts)


def _pack_state(wkv):
    bsz = wkv.shape[0]
    npk = RW_HEADS // HEADS_PER_PACK
    st = jnp.swapaxes(wkv, -1, -2).reshape(bsz, npk, HEADS_PER_PACK, RW_HD, RW_HD)
    eye = jnp.eye(HEADS_PER_PACK, dtype=wkv.dtype)
    return jnp.einsum('bjhkv,hg->bjhkgv', st, eye).reshape(bsz, npk, PACK, PACK)


def _unpack_state(sp):
    bsz, npk = sp.shape[:2]
    s6 = sp.reshape(bsz, npk, HEADS_PER_PACK, RW_HD, HEADS_PER_PACK, RW_HD)
    blocks = jnp.stack([s6[:, :, h, :, h, :] for h in range(HEADS_PER_PACK)], axis=2)
    return jnp.swapaxes(blocks.reshape(bsz, RW_HEADS, RW_HD, RW_HD), -1, -2)


def _merge_kernel(x_ref, op_ref, oa_ref, or_ref, gt_ref, ga_ref, sh_ref, sc_ref, g2_ref,
                  pp_ref, pa_ref, pr_ref, wo_ref, x1_ref, h2_ref):
    bb, tt, d = x_ref.shape
    m = bb * tt
    gates = gt_ref[...].reshape(m, GATE_COLS)
    merged = None
    for n, (o_ref, w_ref) in enumerate(((op_ref, pp_ref), (oa_ref, pa_ref), (or_ref, pr_ref))):
        o = o_ref[...].reshape(m, o_ref.shape[2])
        term = gates[:, n * d:(n + 1) * d].astype(F32) * jnp.dot(o, w_ref[...], preferred_element_type=F32)
        merged = term if merged is None else merged + term
    y = _dot(merged, wo_ref[...]).reshape(bb, tt, d)
    x1 = x_ref[...] + ga_ref[...] * y
    x1_ref[...] = x1
    ms = jnp.mean(x1 * x1, axis=-1, keepdims=True)
    h = x1 * lax.rsqrt(ms + NORM_EPS) * g2_ref[...]
    h2_ref[...] = (h * (1.0 + sc_ref[...]) + sh_ref[...]).astype(BF16)


def _merge(x, o_pool, o_attn, o_rw, gates, ga1, sh2, sc2, g2, pp, pa, pr, wo):
    bsz, seq, d = x.shape
    bb, tt = _tile_rows(bsz, seq, 512)
    tok = lambda f: pl.BlockSpec((bb, tt, f), lambda b, i: (b, i, 0))
    mod = pl.BlockSpec((bb, 1, d), lambda b, i: (b, 0, 0))
    full = lambda a: pl.BlockSpec(a.shape, lambda b, i: (0,) * a.ndim)
    return pl.pallas_call(
        _merge_kernel,
        grid=(bsz // bb, seq // tt),
        in_specs=[tok(d), tok(POOL_WIDTH), tok(ATTN_WIDTH), tok(RW_WIDTH), tok(GATE_COLS),
                  mod, mod, mod, full(g2), full(pp), full(pa), full(pr), full(wo)],
        out_specs=[tok(d), tok(d)],
        out_shape=[jax.ShapeDtypeStruct((bsz, seq, d), F32), jax.ShapeDtypeStruct((bsz, seq, d), BF16)],
        compiler_params=_cparams(("parallel", "parallel")),
        name="merge",
    )(x, o_pool, o_attn, o_rw, gates, ga1, sh2, sc2, g2, pp, pa, pr, wo)


def _swiglu_rows(h, w1, w3, w2):
    a = jnp.dot(h, w1, preferred_element_type=F32)
    b = jnp.dot(h, w3, preferred_element_type=F32)
    return jnp.dot((a * _sigmoid(a) * b).astype(BF16), w2, preferred_element_type=F32)


def _ffn_kernel(x_ref, h_ref, ga_ref, w1_ref, w3_ref, w2_ref, o_ref, *, n_split):
    bb, tt, d = x_ref.shape
    h = h_ref[...].reshape(bb * tt, d)
    ff = w1_ref.shape[1]
    step = ff // n_split
    f = None
    for s in range(n_split):
        sl = slice(s * step, (s + 1) * step)
        part = _swiglu_rows(h, w1_ref[:, sl], w3_ref[:, sl], w2_ref[sl, :])
        f = part if f is None else f + part
    o_ref[...] = x_ref[...] + ga_ref[...] * f.reshape(bb, tt, d)


def _ffn(x, h, ga, w1, w3, w2):
    bsz, seq, d = x.shape
    bb, tt = _tile_rows(bsz, seq, 512)
    tok = pl.BlockSpec((bb, tt, d), lambda b, i: (b, i, 0))
    mod = pl.BlockSpec((bb, 1, d), lambda b, i: (b, 0, 0))
    full = lambda a: pl.BlockSpec(a.shape, lambda b, i: (0,) * a.ndim)
    return pl.pallas_call(
        functools.partial(_ffn_kernel, n_split=2),
        grid=(bsz // bb, seq // tt),
        in_specs=[tok, tok, mod, full(w1), full(w3), full(w2)],
        out_specs=tok,
        out_shape=jax.ShapeDtypeStruct((bsz, seq, d), F32),
        compiler_params=_cparams(("parallel", "parallel")),
        name="ffn",
    )(x, h, ga, w1, w3, w2)


def _moe_kernel(x_ref, h_ref, ga_ref, rw_ref, rb_ref, w1_ref, w3_ref, w2_ref, o_ref, gate_scr):
    e = pl.program_id(2)
    bb, tt, d = x_ref.shape
    m = bb * tt
    h = h_ref[...].reshape(m, d)
    lane = lax.broadcasted_iota(jnp.int32, (m, LANES), 1)

    @pl.when(e == 0)
    def _():
        logits = _dot(h, rw_ref[...]) + rb_ref[...]
        logits = jnp.where(lane < N_EXPERTS, logits, NEG_BIG)
        mx = jnp.max(logits, axis=-1, keepdims=True)
        ex = jnp.exp(logits - mx)
        probs = ex / jnp.sum(ex, axis=-1, keepdims=True)
        p1 = jnp.max(probs, axis=-1, keepdims=True)
        i1 = jnp.min(jnp.where(probs == p1, lane, LANES), axis=-1, keepdims=True)
        rest = jnp.where(lane == i1, -1.0, probs)
        p2 = jnp.max(rest, axis=-1, keepdims=True)
        i2 = jnp.min(jnp.where(rest == p2, lane, LANES), axis=-1, keepdims=True)
        tot = p1 + p2
        gate_scr[...] = jnp.where(lane == i1, p1 / tot, 0.0) + jnp.where(lane == i2, p2 / tot, 0.0)
        o_ref[...] = x_ref[...]

    gate = jnp.sum(jnp.where(lane == e, gate_scr[...], 0.0), axis=-1, keepdims=True)
    f = _swiglu_rows(h, w1_ref[0], w3_ref[0], w2_ref[0])
    o_ref[...] += ga_ref[...] * (gate * f).reshape(bb, tt, d)


def _moe(x, h, ga, rw_p, rb_p, w1, w3, w2):
    bsz, seq, d = x.shape
    bb, tt = _tile_rows(bsz, seq, 1024)
    n_e, _, ffe = w1.shape
    tok = pl.BlockSpec((bb, tt, d), lambda b, i, e: (b, i, 0))
    mod = pl.BlockSpec((bb, 1, d), lambda b, i, e: (b, 0, 0))
    full = lambda a: pl.BlockSpec(a.shape, lambda b, i, e: (0,) * a.ndim)
    return pl.pallas_call(
        _moe_kernel,
        grid=(bsz // bb, seq // tt, n_e),
        in_specs=[tok, tok, mod, full(rw_p), full(rb_p),
                  pl.BlockSpec((1, d, ffe), lambda b, i, e: (e, 0, 0)),
                  pl.BlockSpec((1, d, ffe), lambda b, i, e: (e, 0, 0)),
                  pl.BlockSpec((1, ffe, d), lambda b, i, e: (e, 0, 0))],
        out_specs=tok,
        out_shape=jax.ShapeDtypeStruct((bsz, seq, d), F32),
        scratch_shapes=[pltpu.VMEM((bb * tt, LANES), F32)],
        compiler_params=_cparams(("parallel", "parallel", "arbitrary")),
        name="moe",
    )(x, h, ga, rw_p, rb_p, w1, w3, w2)


def _prep_weights(W):
    depth = W['w_in'].shape[0]
    P = {}
    w_in = W['w_in']
    pad = jnp.zeros((depth, D_MODEL, RW_PAD - RW_IN), w_in.dtype)
    P['w_in'] = jnp.concatenate([w_in[:, :, :OFF_GATE], pad, w_in[:, :, OFF_GATE:]], axis=-1).astype(BF16)
    P['qg'] = jnp.tile(W['q_norm_g'], (1, N_HEADS)).reshape(depth, 1, ATTN_WIDTH)
    P['kg'] = jnp.tile(W['k_norm_g'], (1, N_KV)).reshape(depth, 1, KV_WIDTH)
    P['g1'] = W['norm1_g'].reshape(depth, 1, D_MODEL)
    P['g2'] = W['norm2_g'].reshape(depth, 1, D_MODEL)
    P['pool_w'] = W['pool_w'].astype(BF16)
    P['pool_scale'] = W['pool_scale'].reshape(depth, 1, POOL_WIDTH)
    P['mu'] = jnp.pad(W['rw_mu'], ((0, 0), (0, RW_PAD - RW_IN))).reshape(depth, 1, RW_PAD)
    vec = jnp.stack([W['rw_w0'], W['rw_a0'], W['rw_k_k'], W['rw_k_a'],
                     W['rw_r_k'].reshape(depth, RW_WIDTH), W['rw_ln_w'], W['rw_ln_b'],
                     jnp.zeros_like(W['rw_w0'])], axis=1)
    P['vec'] = vec
    zl = jnp.zeros_like(W['rw_w2'])
    P['wwa'] = jnp.concatenate([jnp.concatenate([W['rw_w2'], zl], axis=2),
                                jnp.concatenate([zl, W['rw_a2']], axis=2)], axis=1).astype(BF16)
    P['g2p'] = jnp.pad(W['rw_g2'], ((0, 0), (0, RW_PAD - RW_GATE_OFF - RW_GATE_LORA), (0, 0))).astype(BF16)
    for name in ('proj_pool', 'proj_attn', 'proj_rwkv', 'w_out', 'ffn_w1', 'ffn_w3', 'ffn_w2',
                 'moe_w1', 'moe_w3', 'moe_w2'):
        P[name] = W[name].astype(BF16)
    P['router_w'] = jnp.pad(W['router_w'], ((0, 0), (0, 0), (0, LANES - N_EXPERTS))).astype(BF16)
    P['router_b'] = jnp.pad(W['router_b'], ((0, 0), (0, LANES - N_EXPERTS))).reshape(-1, 1, LANES)
    P['sink'] = W['attn_sink']
    return P


def _layer(x, mods, l, P, pos0, hist_pool, hist_shift, wkv0, ck, cv, chunks_per_tile):
    bsz, seq, _ = x.shape
    sh1, sc1, ga1, sh2, sc2, ga2 = mods
    u, q, k, v, p, gates = _inproj(x, sh1, sc1, P['g1'][l], P['w_in'][l], P['qg'][l], P['kg'][l])
    hist16 = jnp.pad(hist_pool, ((0, 0), (1, 0), (0, 0)))
    o_pool = _pool(u, hist16, P['pool_w'][l], P['pool_scale'][l], pos0)
    if ck is None:
        o_attn = _attn_banded(q, k, v, P['sink'][l])
        new_k, new_v = k[:, -WINDOW:], v[:, -WINDOW:]
    else:
        o_attn = _attn_cached(q, k, v, ck.reshape(bsz, -1, KV_WIDTH), cv.reshape(bsz, -1, KV_WIDTH), P['sink'][l])
        new_k, new_v = k, v
    hist_p = jnp.pad(hist_shift, ((0, 0), (0, 0), (0, RW_PAD - RW_IN)))
    o_rw, s_out = _rwkv(p, hist_p, _pack_state(wkv0), P['mu'][l], P['vec'][l], P['wwa'][l], P['g2p'][l],
                        chunks_per_tile)
    x1, h2 = _merge(x, o_pool, o_attn, o_rw, gates, ga1, sh2, sc2, P['g2'][l],
                    P['proj_pool'][l], P['proj_attn'][l], P['proj_rwkv'][l], P['w_out'][l])
    i = l // 2
    if l % 2 == 0:
        x2 = _ffn(x1, h2, ga2, P['ffn_w1'][i], P['ffn_w3'][i], P['ffn_w2'][i])
    else:
        x2 = _moe(x1, h2, ga2, P['router_w'][i], P['router_b'][i], P['moe_w1'][i], P['moe_w3'][i], P['moe_w2'][i])
    new_pool = jnp.concatenate([hist_pool, u], axis=1)[:, -POOL_HIST:]
    new = (new_k.reshape(bsz, -1, N_KV, HEAD_DIM), new_v.reshape(bsz, -1, N_KV, HEAD_DIM),
           new_pool, p[:, -1:, :RW_IN], _unpack_state(s_out))
    return x2, new


def _trunk(x, mod_all, P, pos0, states, chunks_per_tile):
    bsz = x.shape[0]
    depth = P['w_in'].shape[0]
    outs = ([], [], [], [], [])
    for l in range(depth):
        mods = tuple(mod_all[l, :, n * D_MODEL:(n + 1) * D_MODEL].reshape(bsz, 1, D_MODEL) for n in range(6))
        if states is None:
            hp = jnp.zeros((bsz, POOL_HIST, POOL_WIDTH), F32)
            hs = jnp.zeros((bsz, 1, RW_IN), F32)
            s0 = jnp.zeros((bsz, RW_HEADS, RW_HD, RW_HD), F32)
            ck = cv = None
        else:
            ck, cv, hp, hs, s0 = (s[l] for s in states)
        x, new = _layer(x, mods, l, P, pos0, hp, hs, s0, ck, cv, chunks_per_tile)
        for lst, arr in zip(outs, new):
            lst.append(arr)
    return x, [jnp.stack(lst) for lst in outs]


def kernel(x_prompt, x_sample, c_prompt, c_sample, cache_attn_k, cache_attn_v, state_pool, state_rwkv_shift, state_rwkv_wkv, norm1_g, norm2_g, ada_w, ada_b, w_in, pool_w, pool_scale, q_norm_g, k_norm_g, attn_sink, rw_mu, rw_w0, rw_w2, rw_a0, rw_a2, rw_g2, rw_k_k, rw_k_a, rw_r_k, rw_ln_w, rw_ln_b, proj_pool, proj_attn, proj_rwkv, w_out, ffn_w1, ffn_w3, ffn_w2, router_w, router_b, moe_w1, moe_w3, moe_w2):
    W = dict(norm1_g=norm1_g, norm2_g=norm2_g, w_in=w_in, pool_w=pool_w,
             pool_scale=pool_scale, q_norm_g=q_norm_g, k_norm_g=k_norm_g, attn_sink=attn_sink,
             rw_mu=rw_mu, rw_w0=rw_w0, rw_w2=rw_w2, rw_a0=rw_a0, rw_a2=rw_a2, rw_g2=rw_g2,
             rw_k_k=rw_k_k, rw_k_a=rw_k_a, rw_r_k=rw_r_k, rw_ln_w=rw_ln_w, rw_ln_b=rw_ln_b,
             proj_pool=proj_pool, proj_attn=proj_attn, proj_rwkv=proj_rwkv, w_out=w_out,
             ffn_w1=ffn_w1, ffn_w3=ffn_w3, ffn_w2=ffn_w2, router_w=router_w, router_b=router_b,
             moe_w1=moe_w1, moe_w3=moe_w3, moe_w2=moe_w2)
    P = _prep_weights(W)
    n_prompt = c_prompt.shape[0]
    mod_all = _ada_mod(jnp.concatenate([c_prompt, c_sample], axis=0), ada_w, ada_b)
    y_prompt, p_new = _trunk(x_prompt, mod_all[:, :n_prompt], P, 0, None, RWKV_TILE_CHUNKS)
    y_sample, s_new = _trunk(x_sample, mod_all[:, n_prompt:], P, PAST_LEN,
                             (cache_attn_k, cache_attn_v, state_pool, state_rwkv_shift, state_rwkv_wkv), 1)
    return (y_prompt, y_sample, *p_new, *s_new)
```

```python
import functools

import jax
import jax.numpy as jnp
from jax import lax
from jax.experimental import pallas as pl
from jax.experimental.pallas import tpu as pltpu

F32 = jnp.float32
BF16 = jnp.bfloat16

D_MODEL = 1024
CHUNK = 64
NORM_EPS = 1e-6
N_BRANCH = 3
POOL_WINDOWS = (2, 4, 8, 16)
POOL_GROUP = 128
POOL_WIDTH = 512
POOL_HIST = 15
HEAD_DIM = 64
N_HEADS = 8
N_KV = 2
ATTN_WIDTH = 512
KV_WIDTH = 128
WINDOW = 128
ATTN_SCALE = HEAD_DIM ** -0.5
RW_HD = 64
RW_HEADS = 8
RW_WIDTH = 512
RW_DECAY_LORA = 64
RW_A_LORA = 64
RW_GATE_LORA = 160
RW_IN = 3 * RW_WIDTH + RW_DECAY_LORA + RW_A_LORA + RW_GATE_LORA
RW_GN_EPS = 64e-5
OFF_Q = POOL_WIDTH
OFF_K = OFF_Q + ATTN_WIDTH
OFF_V = OFF_K + KV_WIDTH
OFF_RW = OFF_V + KV_WIDTH
OFF_GATE = OFF_RW + RW_IN
N_EXPERTS = 8
PAST_LEN = 4096

LANES = 128
RW_PAD = 1920
RW_LORA_OFF = 3 * RW_WIDTH
RW_GATE_OFF = RW_LORA_OFF + LANES
GATE_COLS = N_BRANCH * D_MODEL
PAD_GATE = OFF_RW + RW_PAD
IN_COLS_PAD = PAD_GATE + GATE_COLS
HEADS_PER_PACK = 4
PACK = HEADS_PER_PACK * RW_HD
RWKV_TILE_CHUNKS = 4
RWKV_WAVE_UNITS = 8
ATTN_TILE_CHUNKS = 4
MOE_TILE = 1024
MOE_ROWS = 288
NEG_BIG = -1e30
VMEM_LIMIT = 56 * 1024 * 1024


def _cparams(sem):
    return pltpu.CompilerParams(dimension_semantics=sem, vmem_limit_bytes=VMEM_LIMIT)


def _dot(a, b):
    return jnp.dot(a.astype(BF16), b.astype(BF16), preferred_element_type=F32)


def _dot_nt(a, b):
    return lax.dot_general(a.astype(BF16), b.astype(BF16), (((1,), (1,)), ((), ())),
                           preferred_element_type=F32)


def _dot_tn(a, b):
    return lax.dot_general(a.astype(BF16), b.astype(BF16), (((0,), (0,)), ((), ())),
                           preferred_element_type=F32)


def _split_bf16(x):
    hi = x.astype(BF16)
    return hi, (x - hi.astype(F32)).astype(BF16)


def _sigmoid(x):
    return 1.0 / (1.0 + jnp.exp(-x))


def _group_ones(width, group):
    r = lax.broadcasted_iota(jnp.int32, (width, width), 0) // group
    c = lax.broadcasted_iota(jnp.int32, (width, width), 1) // group
    return (r == c).astype(BF16)


def _resident(a):
    return pl.BlockSpec(a.shape, lambda *_: (0,) * a.ndim, pipeline_mode=pl.Buffered(1))


def _tile_rows(bsz, seq, rows):
    if seq >= rows:
        assert seq % rows == 0
        return 1, rows
    bb = max(1, min(bsz, rows // seq))
    while bsz % bb:
        bb -= 1
    return bb, seq


def _mod_kernel(c_ref, w_ref, b_ref, o_ref):
    c = c_ref[...]
    o_ref[0] = _dot(c * _sigmoid(c), w_ref[0]) + b_ref[0]


def _ada_mod(c_all, ada_w, ada_b):
    depth, d, cols = ada_w.shape
    rows = c_all.shape[0]
    tn = 2048
    return pl.pallas_call(
        _mod_kernel,
        grid=(depth, cols // tn),
        in_specs=[pl.BlockSpec((rows, d), lambda l, j: (0, 0)),
                  pl.BlockSpec((1, d, tn), lambda l, j: (l, 0, j)),
                  pl.BlockSpec((1, 1, tn), lambda l, j: (l, 0, j))],
        out_specs=pl.BlockSpec((1, rows, tn), lambda l, j: (l, 0, j)),
        out_shape=jax.ShapeDtypeStruct((depth, rows, cols), F32),
        compiler_params=_cparams(("parallel", "parallel")),
        name="ada_mod",
    )(c_all, ada_w, ada_b.reshape(depth, 1, cols))


def _inproj_kernel(x_ref, sh_ref, sc_ref, g_ref, w_ref, qg_ref, kg_ref,
                   u_ref, q_ref, k_ref, v_ref, p_ref, gt_ref):
    bb, tt, d = x_ref.shape
    m = bb * tt
    x = x_ref[...]
    ms = jnp.mean(x * x, axis=-1, keepdims=True)
    h = x * lax.rsqrt(ms + NORM_EPS) * g_ref[...]
    h = h * (1.0 + sc_ref[...]) + sh_ref[...]
    hb = h.astype(BF16).reshape(m, d)

    u_ref[...] = jnp.dot(hb, w_ref[:, 0:OFF_Q], preferred_element_type=F32).reshape(bb, tt, POOL_WIDTH)

    zq = jnp.dot(hb, w_ref[:, OFF_Q:OFF_K], preferred_element_type=F32)
    msq = _dot(zq * zq, _group_ones(ATTN_WIDTH, HEAD_DIM)) * (1.0 / HEAD_DIM)
    qn = zq * lax.rsqrt(msq + NORM_EPS) * qg_ref[...] * ATTN_SCALE
    q_ref[...] = qn.astype(BF16).reshape(bb, tt, ATTN_WIDTH)

    zk = jnp.dot(hb, w_ref[:, OFF_K:OFF_V], preferred_element_type=F32)
    msk = _dot(zk * zk, _group_ones(KV_WIDTH, HEAD_DIM)) * (1.0 / HEAD_DIM)
    k_ref[...] = (zk * lax.rsqrt(msk + NORM_EPS) * kg_ref[...]).reshape(bb, tt, KV_WIDTH)

    v_ref[...] = jnp.dot(hb, w_ref[:, OFF_V:OFF_RW], preferred_element_type=F32).reshape(bb, tt, KV_WIDTH)

    p_ref[...] = jnp.dot(hb, w_ref[:, OFF_RW:PAD_GATE], preferred_element_type=F32).reshape(bb, tt, RW_PAD)

    zg = jnp.dot(hb, w_ref[:, PAD_GATE:IN_COLS_PAD], preferred_element_type=F32)
    gt_ref[...] = _sigmoid(zg).astype(BF16).reshape(bb, tt, GATE_COLS)


def _inproj(x, sh, sc, g1, w_in_p, qg, kg):
    bsz, seq, d = x.shape
    bb, tt = _tile_rows(bsz, seq, 512)
    tok = lambda f: pl.BlockSpec((bb, tt, f), lambda b, i: (b, i, 0))
    mod = pl.BlockSpec((bb, 1, d), lambda b, i: (b, 0, 0))
    full = _resident
    return pl.pallas_call(
        _inproj_kernel,
        grid=(bsz // bb, seq // tt),
        in_specs=[tok(d), mod, mod, full(g1), full(w_in_p), full(qg), full(kg)],
        out_specs=[tok(POOL_WIDTH), tok(ATTN_WIDTH), tok(KV_WIDTH), tok(KV_WIDTH), tok(RW_PAD), tok(GATE_COLS)],
        out_shape=[jax.ShapeDtypeStruct((bsz, seq, POOL_WIDTH), F32),
                   jax.ShapeDtypeStruct((bsz, seq, ATTN_WIDTH), BF16),
                   jax.ShapeDtypeStruct((bsz, seq, KV_WIDTH), F32),
                   jax.ShapeDtypeStruct((bsz, seq, KV_WIDTH), F32),
                   jax.ShapeDtypeStruct((bsz, seq, RW_PAD), F32),
                   jax.ShapeDtypeStruct((bsz, seq, GATE_COLS), BF16)],
        compiler_params=_cparams(("parallel", "parallel")),
        name="inproj",
    )(x, sh, sc, g1, w_in_p, qg, kg)


def _pool_kernel(u_ref, uprev_ref, hist_ref, pw_ref, ps_ref, o_ref, *, pos0):
    i = pl.program_id(1)
    tt = u_ref.shape[1]
    halo = uprev_ref.shape[1]
    u = u_ref[0]
    prev = jnp.where(i == 0, hist_ref[0], uprev_ref[0])
    ext = jnp.concatenate([prev, u], axis=0)
    sums = {}
    acc = ext
    for w in POOL_WINDOWS:
        acc = acc + pltpu.roll(acc, w // 2, axis=0)
        sums[w] = acc
    pos = pos0 + i * tt + lax.broadcasted_iota(jnp.int32, (tt, 1), 0)
    outs = []
    for g, w in enumerate(POOL_WINDOWS):
        sl = slice(g * POOL_GROUP, (g + 1) * POOL_GROUP)
        cnt = jnp.minimum(pos + 1, w).astype(F32)
        dlt = sums[w][halo:, sl] / cnt - u[:, sl]
        outs.append(_dot(dlt, pw_ref[g]))
    y = jnp.concatenate(outs, axis=1) * ps_ref[...]
    o_ref[0] = y.astype(BF16)


def _pool(u, hist16, pool_w, pool_scale, pos0):
    bsz, seq, _ = u.shape
    tt = min(seq, 512)
    halo = hist16.shape[1]
    per = tt // halo
    return pl.pallas_call(
        functools.partial(_pool_kernel, pos0=pos0),
        grid=(bsz, seq // tt),
        in_specs=[pl.BlockSpec((1, tt, POOL_WIDTH), lambda b, i: (b, i, 0)),
                  pl.BlockSpec((1, halo, POOL_WIDTH), lambda b, i: (b, jnp.maximum(i * per - 1, 0), 0)),
                  pl.BlockSpec((1, halo, POOL_WIDTH), lambda b, i: (b, 0, 0)),
                  pl.BlockSpec(pool_w.shape, lambda b, i: (0, 0, 0)),
                  pl.BlockSpec(pool_scale.shape, lambda b, i: (0, 0))],
        out_specs=pl.BlockSpec((1, tt, POOL_WIDTH), lambda b, i: (b, i, 0)),
        out_shape=jax.ShapeDtypeStruct((bsz, seq, POOL_WIDTH), BF16),
        compiler_params=_cparams(("parallel", "arbitrary")),
        name="pool",
    )(u, u, hist16, pool_w, pool_scale)


def _attn_core(q, kcat, vcat, bias, sink_ref):
    lo = lax.broadcasted_iota(jnp.int32, (1, KV_WIDTH), 1) < HEAD_DIM
    krot = pltpu.roll(kcat, HEAD_DIM, axis=1)
    vrot = pltpu.roll(vcat, HEAD_DIM, axis=1)
    khs, vhs, qps, sinks = [], [], [], []
    for g in range(N_KV):
        ksrc_lo, ksrc_hi = (kcat, krot) if g == 0 else (krot, kcat)
        vsrc_lo, vsrc_hi = (vcat, vrot) if g == 0 else (vrot, vcat)
        k_lo = jnp.where(lo, ksrc_lo, 0.0).astype(BF16)
        k_hi = jnp.where(lo, 0.0, ksrc_hi).astype(BF16)
        v_lo = jnp.where(lo, vsrc_lo, 0.0).astype(BF16)
        v_hi = jnp.where(lo, 0.0, vsrc_hi).astype(BF16)
        for jj in range(2):
            pair = 2 * g + jj
            qp = q[:, pair * LANES:(pair + 1) * LANES]
            for half, (kh, vh) in enumerate(((k_lo, v_lo), (k_hi, v_hi))):
                khs.append(kh)
                vhs.append(vh)
                qps.append(qp)
                sinks.append(sink_ref[2 * pair + half])
    ss = [_dot_nt(qp, kh) + bias for qp, kh in zip(qps, khs)]
    mxs = [jnp.maximum(jnp.max(s, axis=-1, keepdims=True), sk) for s, sk in zip(ss, sinks)]
    prs = [jnp.exp(s - mx) for s, mx in zip(ss, mxs)]
    inv = [1.0 / (jnp.sum(pr, axis=-1, keepdims=True) + jnp.exp(sk - mx))
           for pr, sk, mx in zip(prs, sinks, mxs)]
    ohs = [jnp.dot(pr.astype(BF16), vh, preferred_element_type=F32) * iv
           for pr, vh, iv in zip(prs, vhs, inv)]
    return jnp.concatenate([ohs[2 * p] + ohs[2 * p + 1] for p in range(N_HEADS // 2)], axis=1)


def _attn_banded_kernel(sink_ref, q_ref, kp_ref, kc_ref, vp_ref, vc_ref, band_ref, o_ref):
    i = pl.program_id(1)
    kcat = jnp.concatenate([kp_ref[0], kc_ref[0]], axis=0)
    vcat = jnp.concatenate([vp_ref[0], vc_ref[0]], axis=0)
    col = lax.broadcasted_iota(jnp.int32, (1, kcat.shape[0]), 1)
    first = jnp.where((col < WINDOW) & (i == 0), NEG_BIG, 0.0)
    o_ref[0] = _attn_core(q_ref[0], kcat, vcat, band_ref[...] + first, sink_ref).astype(BF16)


def _band_bias(tq):
    qc = jnp.arange(tq)[:, None] // CHUNK
    kc = jnp.arange(WINDOW + tq)[None, :] // CHUNK
    win = WINDOW // CHUNK
    return jnp.where((kc >= qc) & (kc <= qc + win), 0.0, NEG_BIG).astype(F32)


def _attn_banded(q, k, v, sink):
    bsz, seq, _ = q.shape
    tq = min(seq, ATTN_TILE_CHUNKS * CHUNK)
    assert seq % tq == 0 and tq % WINDOW == 0
    per = tq // WINDOW
    band = _band_bias(tq)
    prev = pl.BlockSpec((1, WINDOW, KV_WIDTH), lambda b, i: (b, jnp.maximum(i * per - 1, 0), 0))
    cur = pl.BlockSpec((1, tq, KV_WIDTH), lambda b, i: (b, i, 0))
    return pl.pallas_call(
        _attn_banded_kernel,
        grid=(bsz, seq // tq),
        in_specs=[pl.BlockSpec(memory_space=pltpu.SMEM),
                  pl.BlockSpec((1, tq, ATTN_WIDTH), lambda b, i: (b, i, 0)),
                  prev, cur, prev, cur,
                  pl.BlockSpec(band.shape, lambda b, i: (0, 0))],
        out_specs=pl.BlockSpec((1, tq, ATTN_WIDTH), lambda b, i: (b, i, 0)),
        out_shape=jax.ShapeDtypeStruct((bsz, seq, ATTN_WIDTH), BF16),
        compiler_params=_cparams(("parallel", "parallel")),
        name="attn_banded",
    )(sink, q, k, k, v, v, band)


def _attn_cached_kernel(sink_ref, q_ref, ck_ref, k_ref, cv_ref, v_ref, o_ref):
    cache, new = ck_ref.shape[1], k_ref.shape[1]
    span = 3 * CHUNK
    pad = jnp.zeros((span - cache - new, KV_WIDTH), F32)
    kcat = jnp.concatenate([ck_ref[0], k_ref[0], pad], axis=0)
    vcat = jnp.concatenate([cv_ref[0], v_ref[0], pad], axis=0)
    col = lax.broadcasted_iota(jnp.int32, (1, span), 1)
    bias = jnp.where(col < cache + new, 0.0, NEG_BIG)
    o_ref[0] = _attn_core(q_ref[0], kcat, vcat, bias, sink_ref).astype(BF16)


def _attn_cached(q, k, v, ck, cv, sink):
    bsz, seq, _ = q.shape
    cache = ck.shape[1]
    assert cache + seq <= 3 * CHUNK and (3 * CHUNK - cache - seq) % 8 == 0
    blk = lambda t, f: pl.BlockSpec((1, t, f), lambda b: (b, 0, 0))
    return pl.pallas_call(
        _attn_cached_kernel,
        grid=(bsz,),
        in_specs=[pl.BlockSpec(memory_space=pltpu.SMEM), blk(seq, ATTN_WIDTH),
                  blk(cache, KV_WIDTH), blk(seq, KV_WIDTH), blk(cache, KV_WIDTH), blk(seq, KV_WIDTH)],
        out_specs=blk(seq, ATTN_WIDTH),
        out_shape=jax.ShapeDtypeStruct((bsz, seq, ATTN_WIDTH), BF16),
        compiler_params=_cparams(("parallel",)),
        name="attn_cached",
    )(sink, q, ck, k, cv, v)


def _rwkv_kernel(p_ref, pprev_ref, hist_ref, s0_ref, mu_ref, vec_ref, wwa_ref, g2_ref,
                 ones_ref, mbd_ref, mbd2_ref, tri_ref, mst_ref,
                 o_ref, sout_ref, s_scr, *, chunk):
    i = pl.program_id(1)
    tt = p_ref.shape[1]
    c_len = chunk
    n_chunks = tt // c_len
    hp = HEADS_PER_PACK
    wide = hp * c_len
    n_pack = RW_WIDTH // PACK

    @pl.when(i == 0)
    def _():
        s_scr[...] = s0_ref[0]

    p = p_ref[0]
    prev_row = jnp.where(i == 0, hist_ref[0], pprev_ref[0, 7:8, :])
    row = lax.broadcasted_iota(jnp.int32, (tt, 1), 0)
    prev = jnp.where(row == 0, prev_row, pltpu.roll(p, 1, axis=0))
    pm = p + (prev - p) * mu_ref[...]

    w0, a0, k_k, k_a = vec_ref[0:1], vec_ref[1:2], vec_ref[2:3], vec_ref[3:4]
    r_k, ln_w, ln_b = vec_ref[4:5], vec_ref[5:6], vec_ref[6:7]
    r = pm[:, 0:RW_WIDTH]
    k = pm[:, RW_WIDTH:2 * RW_WIDTH]
    v = pm[:, 2 * RW_WIDTH:3 * RW_WIDTH]
    xa = pm[:, RW_LORA_OFF:RW_LORA_OFF + LANES]
    lane = lax.broadcasted_iota(jnp.int32, (1, LANES), 1)
    xa = jnp.where(lane < RW_DECAY_LORA, jnp.tanh(xa), xa)
    la = _dot(xa, wwa_ref[...])
    wlin = -(w0 + la[:, :RW_WIDTH])
    softplus = jnp.maximum(wlin, 0.0) + jnp.log(1.0 + jnp.exp(-jnp.abs(wlin)))
    lw = -jnp.exp(-softplus - 0.5)
    a = _sigmoid(a0 + la[:, RW_WIDTH:])
    g = _dot(_sigmoid(pm[:, RW_GATE_OFF:RW_PAD]), g2_ref[...])
    ones_g = ones_ref[...]
    kk = k * k_k
    kk = kk / jnp.maximum(jnp.sqrt(_dot(kk * kk, ones_g)), 1e-12)
    k2 = k * (1.0 + (a - 1.0) * k_a)
    bonus = _dot(r * k2 * r_k, ones_g) * v
    kka = kk * a

    in_chunk = row % c_len
    cs = lw
    sft = 1
    while sft < c_len:
        cs = cs + jnp.where(in_chunk >= sft, pltpu.roll(cs, sft, axis=0), 0.0)
        sft *= 2
    tot = jnp.concatenate(
        [jnp.broadcast_to(cs[(c + 1) * c_len - 1:(c + 1) * c_len, :], (c_len, RW_WIDTH)) for c in range(n_chunks)],
        axis=0)
    p_inv = jnp.exp(-cs)
    p_end = jnp.exp(tot - cs)
    at = -kk * jnp.exp(cs - lw)
    bt = kka * p_inv
    kt = k2 * p_inv
    rt = r * jnp.exp(cs)
    bb_ = kka * p_end
    kb = k2 * p_end

    mbd, mbd2 = mbd_ref[...], mbd2_ref[...]
    strict, lower, eye_p = tri_ref[0], tri_ref[1], tri_ref[2]
    mask_state, eye_state = mst_ref[0], mst_ref[1]
    bd = lambda x: jnp.concatenate([x.astype(BF16)] * hp, axis=0) * mbd
    bd2 = lambda x: jnp.concatenate([x.astype(BF16)] * hp, axis=0) * mbd2
    zeros_bd = jnp.zeros((wide, PACK), BF16)
    zeros_cp = jnp.zeros((c_len, PACK), F32)

    all_units = [(c, j) for c in range(n_chunks) for j in range(n_pack)]
    view = lambda x, u: x[u[0] * c_len:(u[0] + 1) * c_len, u[1] * PACK:(u[1] + 1) * PACK]

    def prepass(units):
        s1 = [_dot_nt(jnp.concatenate([view(at, u), view(rt, u)], axis=0),
                      jnp.concatenate([bd(view(bt, u)), bd(view(kt, u))], axis=0)) for u in units]
        l_ab = [s[:c_len, :wide] * strict for s in s1]
        l_ak = [s[:c_len, wide:] * strict for s in s1]
        g_bk = [jnp.concatenate([s[c_len:, :wide] * lower, s[c_len:, wide:] * lower], axis=1) for s in s1]
        pw = [_dot(l, bd2(l)) for l in l_ab]
        tm = [eye_p + l for l in l_ab]
        span = 2
        while span < c_len:
            rhs = [bd2(x) for x in pw]
            if 2 * span < c_len:
                both = [_dot(jnp.concatenate([x, t], axis=0), rr) for x, t, rr in zip(pw, tm, rhs)]
                pw = [bo[:c_len] for bo in both]
                tm = [t + bo[c_len:] for t, bo in zip(tm, both)]
            else:
                tm = [t + _dot(t, rr) for t, rr in zip(tm, rhs)]
            span *= 2
        bdv = [bd(view(v, u)) for u in units]
        x1 = [_dot(l, bv) for l, bv in zip(l_ak, bdv)]
        wu = [_dot(t, jnp.concatenate([bd(view(at, u)), bd(x)], axis=1)) for t, u, x in zip(tm, units, x1)]
        gy = [_dot(gg, jnp.concatenate([jnp.concatenate([bd(w[:, :PACK]), bd(w[:, PACK:])], axis=1),
                                        jnp.concatenate([zeros_bd, bv], axis=1)], axis=0))
              for gg, w, bv in zip(g_bk, wu, bdv)]
        mn = [_dot_tn(jnp.concatenate([view(bb_, u), view(kb, u)], axis=0),
                      jnp.concatenate([w, jnp.concatenate([zeros_cp, view(v, u)], axis=1)], axis=0))
              for u, w in zip(units, wu)]
        return gy, mn

    gy, mn = [], []
    for w0 in range(0, len(all_units), RWKV_WAVE_UNITS):
        gy_w, mn_w = prepass(all_units[w0:w0 + RWKV_WAVE_UNITS])
        gy += gy_w
        mn += mn_w

    y_rows = []
    for c in range(n_chunks):
        y_halves = []
        for j in range(n_pack):
            n = c * n_pack + j
            u = all_units[n]
            p_tot = jnp.exp(tot[c * c_len:c * c_len + 1, j * PACK:(j + 1) * PACK])
            rh = view(rt, u) + gy[n][:, :PACK]
            mc = mn[n][:, :PACK] * mask_state + eye_state * p_tot
            lhs_hi, lhs_lo = _split_bf16(jnp.concatenate([rh, mc], axis=0))
            st_hi, st_lo = _split_bf16(s_scr[j])
            rows = c_len + PACK
            main = jnp.dot(jnp.concatenate([lhs_hi, lhs_lo], axis=0), st_hi, preferred_element_type=F32)
            out = main[:rows] + main[rows:] + jnp.dot(lhs_hi, st_lo, preferred_element_type=F32)
            y_halves.append(out[:c_len] + gy[n][:, PACK:])
            s_scr[j] = out[c_len:] + mn[n][:, PACK:] * mask_state
        y_rows.append(jnp.concatenate(y_halves, axis=1))
    y = y_rows[0] if n_chunks == 1 else jnp.concatenate(y_rows, axis=0)

    inv_hd = 1.0 / RW_HD
    dy = y - _dot(y, ones_g) * inv_hd
    var = _dot(dy * dy, ones_g) * inv_hd
    yn = dy * lax.rsqrt(var + RW_GN_EPS) * ln_w + ln_b
    o_ref[0] = ((yn + bonus) * g).astype(BF16)

    @pl.when(i == pl.num_programs(1) - 1)
    def _():
        sout_ref[0] = s_scr[...]


def _rwkv_masks(c_len):
    wide = HEADS_PER_PACK * c_len
    r_blk = jnp.arange(wide)[:, None] // c_len
    mbd = (r_blk == jnp.arange(PACK)[None, :] // RW_HD).astype(BF16)
    mbd2 = (r_blk == jnp.arange(wide)[None, :] // c_len).astype(BF16)
    t_idx = jnp.arange(c_len)[:, None]
    s_idx = jnp.arange(wide)[None, :] % c_len
    tri = jnp.stack([s_idx < t_idx, s_idx <= t_idx, s_idx == t_idx]).astype(F32)
    st = jnp.arange(PACK)
    mst = jnp.stack([(st[:, None] // RW_HD) == (st[None, :] // RW_HD), st[:, None] == st[None, :]]).astype(F32)
    g_idx = jnp.arange(RW_WIDTH) // RW_HD
    ones_g = (g_idx[:, None] == g_idx[None, :]).astype(BF16)
    return ones_g, mbd, mbd2, tri, mst


def _rwkv(p, hist, s0_packed, mu, vec, wwa, g2p, chunks_per_tile):
    bsz, seq, _ = p.shape
    chunk = min(CHUNK, seq)
    tt = min(seq, chunk * chunks_per_tile)
    per = tt // 8
    npk = RW_WIDTH // PACK
    consts = _rwkv_masks(chunk)
    full = _resident
    return pl.pallas_call(
        functools.partial(_rwkv_kernel, chunk=chunk),
        grid=(bsz, seq // tt),
        in_specs=[pl.BlockSpec((1, tt, RW_PAD), lambda b, i: (b, i, 0)),
                  pl.BlockSpec((1, 8, RW_PAD), lambda b, i: (b, jnp.maximum(i * per - 1, 0), 0)),
                  pl.BlockSpec((1, 1, RW_PAD), lambda b, i: (b, 0, 0)),
                  pl.BlockSpec((1, npk, PACK, PACK), lambda b, i: (b, 0, 0, 0)),
                  full(mu), full(vec), full(wwa), full(g2p)] + [full(c) for c in consts],
        out_specs=[pl.BlockSpec((1, tt, RW_WIDTH), lambda b, i: (b, i, 0)),
                   pl.BlockSpec((1, npk, PACK, PACK), lambda b, i: (b, 0, 0, 0))],
        out_shape=[jax.ShapeDtypeStruct((bsz, seq, RW_WIDTH), BF16),
                   jax.ShapeDtypeStruct((bsz, npk, PACK, PACK), F32)],
        scratch_shapes=[pltpu.VMEM((npk, PACK, PACK), F32)],
        compiler_params=_cparams(("parallel", "arbitrary")),
        name="rwkv",
    )(p, p, hist, s0_packed, mu, vec, wwa, g2p, *consts)


def _pack_state(wkv):
    bsz = wkv.shape[0]
    npk = RW_HEADS // HEADS_PER_PACK
    st = jnp.swapaxes(wkv, -1, -2).reshape(bsz, npk, HEADS_PER_PACK, RW_HD, RW_HD)
    eye = jnp.eye(HEADS_PER_PACK, dtype=wkv.dtype)
    return jnp.einsum('bjhkv,hg->bjhkgv', st, eye).reshape(bsz, npk, PACK, PACK)


def _unpack_state(sp):
    bsz, npk = sp.shape[:2]
    s6 = sp.reshape(bsz, npk, HEADS_PER_PACK, RW_HD, HEADS_PER_PACK, RW_HD)
    blocks = jnp.stack([s6[:, :, h, :, h, :] for h in range(HEADS_PER_PACK)], axis=2)
    return jnp.swapaxes(blocks.reshape(bsz, RW_HEADS, RW_HD, RW_HD), -1, -2)


def _merge_kernel(x_ref, op_ref, oa_ref, or_ref, gt_ref, ga_ref, sh_ref, sc_ref, g2_ref,
                  pp_ref, pa_ref, pr_ref, wo_ref, x1_ref, h2_ref):
    bb, tt, d = x_ref.shape
    m = bb * tt
    gates = gt_ref[...].reshape(m, GATE_COLS)
    merged = None
    for n, (o_ref, w_ref) in enumerate(((op_ref, pp_ref), (oa_ref, pa_ref), (or_ref, pr_ref))):
        o = o_ref[...].reshape(m, o_ref.shape[2])
        term = gates[:, n * d:(n + 1) * d].astype(F32) * jnp.dot(o, w_ref[...], preferred_element_type=F32)
        merged = term if merged is None else merged + term
    y = _dot(merged, wo_ref[...]).reshape(bb, tt, d)
    x1 = x_ref[...] + ga_ref[...] * y
    x1_ref[...] = x1
    ms = jnp.mean(x1 * x1, axis=-1, keepdims=True)
    h = x1 * lax.rsqrt(ms + NORM_EPS) * g2_ref[...]
    h2_ref[...] = (h * (1.0 + sc_ref[...]) + sh_ref[...]).astype(BF16)


def _merge(x, o_pool, o_attn, o_rw, gates, ga1, sh2, sc2, g2, pp, pa, pr, wo):
    bsz, seq, d = x.shape
    bb, tt = _tile_rows(bsz, seq, 512)
    tok = lambda f: pl.BlockSpec((bb, tt, f), lambda b, i: (b, i, 0))
    mod = pl.BlockSpec((bb, 1, d), lambda b, i: (b, 0, 0))
    full = _resident
    return pl.pallas_call(
        _merge_kernel,
        grid=(bsz // bb, seq // tt),
        in_specs=[tok(d), tok(POOL_WIDTH), tok(ATTN_WIDTH), tok(RW_WIDTH), tok(GATE_COLS),
                  mod, mod, mod, full(g2), full(pp), full(pa), full(pr), full(wo)],
        out_specs=[tok(d), tok(d)],
        out_shape=[jax.ShapeDtypeStruct((bsz, seq, d), F32), jax.ShapeDtypeStruct((bsz, seq, d), BF16)],
        compiler_params=_cparams(("parallel", "parallel")),
        name="merge",
    )(x, o_pool, o_attn, o_rw, gates, ga1, sh2, sc2, g2, pp, pa, pr, wo)


def _swiglu_rows(h, w1, w3, w2):
    a = jnp.dot(h, w1, preferred_element_type=F32)
    b = jnp.dot(h, w3, preferred_element_type=F32)
    return jnp.dot((a * _sigmoid(a) * b).astype(BF16), w2, preferred_element_type=F32)


def _ffn_kernel(x_ref, h_ref, ga_ref, w1_ref, w3_ref, w2_ref, o_ref, *, n_split):
    bb, tt, d = x_ref.shape
    h = h_ref[...].reshape(bb * tt, d)
    ff = w1_ref.shape[1]
    step = ff // n_split
    f = None
    for s in range(n_split):
        sl = slice(s * step, (s + 1) * step)
        part = _swiglu_rows(h, w1_ref[:, sl], w3_ref[:, sl], w2_ref[sl, :])
        f = part if f is None else f + part
    o_ref[...] = x_ref[...] + ga_ref[...] * f.reshape(bb, tt, d)


def _ffn(x, h, ga, w1, w3, w2):
    bsz, seq, d = x.shape
    bb, tt = _tile_rows(bsz, seq, 512)
    tok = pl.BlockSpec((bb, tt, d), lambda b, i: (b, i, 0))
    mod = pl.BlockSpec((bb, 1, d), lambda b, i: (b, 0, 0))
    full = _resident
    return pl.pallas_call(
        functools.partial(_ffn_kernel, n_split=2),
        grid=(bsz // bb, seq // tt),
        in_specs=[tok, tok, mod, full(w1), full(w3), full(w2)],
        out_specs=tok,
        out_shape=jax.ShapeDtypeStruct((bsz, seq, d), F32),
        compiler_params=_cparams(("parallel", "parallel")),
        name="ffn",
    )(x, h, ga, w1, w3, w2)


def _router_kernel(h_ref, rw_ref, rb_ref, tri_ref, gate_ref, rank_ref, cnt_ref):
    bb, tt, d = h_ref.shape
    m = bb * tt
    h = h_ref[...].reshape(m, d)
    lane = lax.broadcasted_iota(jnp.int32, (m, LANES), 1)
    logits = _dot(h, rw_ref[...]) + rb_ref[...]
    logits = jnp.where(lane < N_EXPERTS, logits, NEG_BIG)
    mx = jnp.max(logits, axis=-1, keepdims=True)
    ex = jnp.exp(logits - mx)
    probs = ex / jnp.sum(ex, axis=-1, keepdims=True)
    p1 = jnp.max(probs, axis=-1, keepdims=True)
    i1 = jnp.min(jnp.where(probs == p1, lane, LANES), axis=-1, keepdims=True)
    rest = jnp.where(lane == i1, -1.0, probs)
    p2 = jnp.max(rest, axis=-1, keepdims=True)
    i2 = jnp.min(jnp.where(rest == p2, lane, LANES), axis=-1, keepdims=True)
    tot = p1 + p2
    gate = jnp.where(lane == i1, p1 / tot, 0.0) + jnp.where(lane == i2, p2 / tot, 0.0)
    gate_ref[...] = gate.reshape(bb, tt, LANES)
    assigned = jnp.where((lane == i1) | (lane == i2), 1.0, 0.0)
    rr = _dot_tn(assigned, tri_ref[...])
    rank = jnp.where(rr[:, m:] > 0.5, rr[:, :m], -1.0)
    rank_ref[0] = rank[:N_EXPERTS].astype(jnp.int32)
    cnt = jnp.sum(assigned, axis=0, keepdims=True)
    cnt_ref[0] = jnp.broadcast_to(cnt, (8, LANES)).astype(jnp.int32)


def _moe_kernel(cnt_ref, x_ref, h_ref, ga_ref, gate_ref, rank_ref, w1_ref, w3_ref, w2_ref, o_ref):
    e = pl.program_id(2)
    tile = pl.program_id(0) * pl.num_programs(1) + pl.program_id(1)
    bb, tt, d = x_ref.shape
    m = bb * tt

    @pl.when(e == 0)
    def _():
        o_ref[...] = x_ref[...]

    count = cnt_ref[tile * N_EXPERTS + e]
    h = h_ref[...].reshape(m, d)
    lane = lax.broadcasted_iota(jnp.int32, (m, LANES), 1)
    gate = jnp.sum(jnp.where(lane == e, gate_ref[...].reshape(m, LANES), 0.0), axis=-1, keepdims=True)
    rank_row = rank_ref[0, pl.ds(e, 1), :]
    row = lax.broadcasted_iota(jnp.int32, (MOE_ROWS, m), 0)

    def block(kb, carry):
        sel = jnp.where(rank_row == row + kb * MOE_ROWS, 1.0, 0.0).astype(BF16)
        xs = jnp.dot(sel, h, preferred_element_type=F32).astype(BF16)
        f = _swiglu_rows(xs, w1_ref[0], w3_ref[0], w2_ref[0])
        o_ref[...] += ga_ref[...] * (gate * _dot_tn(sel, f)).reshape(bb, tt, d)
        return carry

    lax.fori_loop(0, (count + MOE_ROWS - 1) // MOE_ROWS, block, 0)


def _moe(x, h, ga, rw_p, rb_p, w1, w3, w2):
    bsz, seq, d = x.shape
    bb, tt = _tile_rows(bsz, seq, MOE_TILE)
    m = bb * tt
    nb, nt = bsz // bb, seq // tt
    n_e, _, ffe = w1.shape
    t_i = jnp.arange(m)
    tri = jnp.concatenate([t_i[:, None] < t_i[None, :], t_i[:, None] == t_i[None, :]], axis=1).astype(BF16)
    gate, rank, cnt = pl.pallas_call(
        _router_kernel,
        grid=(nb, nt),
        in_specs=[pl.BlockSpec((bb, tt, d), lambda b, i: (b, i, 0)),
                  pl.BlockSpec(rw_p.shape, lambda b, i: (0, 0)),
                  pl.BlockSpec(rb_p.shape, lambda b, i: (0, 0)),
                  pl.BlockSpec(tri.shape, lambda b, i: (0, 0))],
        out_specs=[pl.BlockSpec((bb, tt, LANES), lambda b, i: (b, i, 0)),
                   pl.BlockSpec((1, N_EXPERTS, m), lambda b, i: (b * nt + i, 0, 0)),
                   pl.BlockSpec((1, 8, LANES), lambda b, i: (b * nt + i, 0, 0))],
        out_shape=[jax.ShapeDtypeStruct((bsz, seq, LANES), F32),
                   jax.ShapeDtypeStruct((nb * nt, N_EXPERTS, m), jnp.int32),
                   jax.ShapeDtypeStruct((nb * nt, 8, LANES), jnp.int32)],
        compiler_params=_cparams(("parallel", "parallel")),
        name="router",
    )(h, rw_p, rb_p, tri)
    counts = cnt[:, 0, :N_EXPERTS].reshape(-1)
    tok = lambda f: pl.BlockSpec((bb, tt, f), lambda b, i, e, c: (b, i, 0))
    return pl.pallas_call(
        _moe_kernel,
        grid_spec=pltpu.PrefetchScalarGridSpec(
            num_scalar_prefetch=1,
            grid=(nb, nt, n_e),
            in_specs=[tok(d), tok(d), pl.BlockSpec((bb, 1, d), lambda b, i, e, c: (b, 0, 0)), tok(LANES),
                      pl.BlockSpec((1, N_EXPERTS, m), lambda b, i, e, c: (b * nt + i, 0, 0)),
                      pl.BlockSpec((1, d, ffe), lambda b, i, e, c: (e, 0, 0)),
                      pl.BlockSpec((1, d, ffe), lambda b, i, e, c: (e, 0, 0)),
                      pl.BlockSpec((1, ffe, d), lambda b, i, e, c: (e, 0, 0))],
            out_specs=tok(d)),
        out_shape=jax.ShapeDtypeStruct((bsz, seq, d), F32),
        compiler_params=_cparams(("parallel", "parallel", "arbitrary")),
        name="moe",
    )(counts, x, h, ga, gate, rank, w1, w3, w2)


def _prep_weights(W):
    depth = W['w_in'].shape[0]
    P = {}
    w_in = W['w_in']
    pad = jnp.zeros((depth, D_MODEL, RW_PAD - RW_IN), w_in.dtype)
    P['w_in'] = jnp.concatenate([w_in[:, :, :OFF_GATE], pad, w_in[:, :, OFF_GATE:]], axis=-1).astype(BF16)
    P['qg'] = jnp.tile(W['q_norm_g'], (1, N_HEADS)).reshape(depth, 1, ATTN_WIDTH)
    P['kg'] = jnp.tile(W['k_norm_g'], (1, N_KV)).reshape(depth, 1, KV_WIDTH)
    P['g1'] = W['norm1_g'].reshape(depth, 1, D_MODEL)
    P['g2'] = W['norm2_g'].reshape(depth, 1, D_MODEL)
    P['pool_w'] = W['pool_w'].astype(BF16)
    P['pool_scale'] = W['pool_scale'].reshape(depth, 1, POOL_WIDTH)
    P['mu'] = jnp.pad(W['rw_mu'], ((0, 0), (0, RW_PAD - RW_IN))).reshape(depth, 1, RW_PAD)
    vec = jnp.stack([W['rw_w0'], W['rw_a0'], W['rw_k_k'], W['rw_k_a'],
                     W['rw_r_k'].reshape(depth, RW_WIDTH), W['rw_ln_w'], W['rw_ln_b'],
                     jnp.zeros_like(W['rw_w0'])], axis=1)
    P['vec'] = vec
    zl = jnp.zeros_like(W['rw_w2'])
    P['wwa'] = jnp.concatenate([jnp.concatenate([W['rw_w2'], zl], axis=2),
                                jnp.concatenate([zl, W['rw_a2']], axis=2)], axis=1).astype(BF16)
    P['g2p'] = jnp.pad(W['rw_g2'], ((0, 0), (0, RW_PAD - RW_GATE_OFF - RW_GATE_LORA), (0, 0))).astype(BF16)
    for name in ('proj_pool', 'proj_attn', 'proj_rwkv', 'w_out', 'ffn_w1', 'ffn_w3', 'ffn_w2',
                 'moe_w1', 'moe_w3', 'moe_w2'):
        P[name] = W[name].astype(BF16)
    P['router_w'] = jnp.pad(W['router_w'], ((0, 0), (0, 0), (0, LANES - N_EXPERTS))).astype(BF16)
    P['router_b'] = jnp.pad(W['router_b'], ((0, 0), (0, LANES - N_EXPERTS))).reshape(-1, 1, LANES)
    P['sink'] = W['attn_sink']
    return P


def _layer(x, mods, l, P, pos0, hist_pool, hist_shift, wkv0, ck, cv, chunks_per_tile):
    bsz, seq, _ = x.shape
    sh1, sc1, ga1, sh2, sc2, ga2 = mods
    u, q, k, v, p, gates = _inproj(x, sh1, sc1, P['g1'][l], P['w_in'][l], P['qg'][l], P['kg'][l])
    hist16 = jnp.pad(hist_pool, ((0, 0), (1, 0), (0, 0)))
    o_pool = _pool(u, hist16, P['pool_w'][l], P['pool_scale'][l], pos0)
    if ck is None:
        o_attn = _attn_banded(q, k, v, P['sink'][l])
        new_k, new_v = k[:, -WINDOW:], v[:, -WINDOW:]
    else:
        o_attn = _attn_cached(q, k, v, ck.reshape(bsz, -1, KV_WIDTH), cv.reshape(bsz, -1, KV_WIDTH), P['sink'][l])
        new_k, new_v = k, v
    hist_p = jnp.pad(hist_shift, ((0, 0), (0, 0), (0, RW_PAD - RW_IN)))
    o_rw, s_out = _rwkv(p, hist_p, _pack_state(wkv0), P['mu'][l], P['vec'][l], P['wwa'][l], P['g2p'][l],
                        chunks_per_tile)
    x1, h2 = _merge(x, o_pool, o_attn, o_rw, gates, ga1, sh2, sc2, P['g2'][l],
                    P['proj_pool'][l], P['proj_attn'][l], P['proj_rwkv'][l], P['w_out'][l])
    i = l // 2
    if l % 2 == 0:
        x2 = _ffn(x1, h2, ga2, P['ffn_w1'][i], P['ffn_w3'][i], P['ffn_w2'][i])
    else:
        x2 = _moe(x1, h2, ga2, P['router_w'][i], P['router_b'][i], P['moe_w1'][i], P['moe_w3'][i], P['moe_w2'][i])
    new_pool = jnp.concatenate([hist_pool, u], axis=1)[:, -POOL_HIST:]
    new = (new_k.reshape(bsz, -1, N_KV, HEAD_DIM), new_v.reshape(bsz, -1, N_KV, HEAD_DIM),
           new_pool, p[:, -1:, :RW_IN], _unpack_state(s_out))
    return x2, new


def _trunk(x, mod_all, P, pos0, states, chunks_per_tile):
    bsz = x.shape[0]
    depth = P['w_in'].shape[0]
    outs = ([], [], [], [], [])
    for l in range(depth):
        mods = tuple(mod_all[l, :, n * D_MODEL:(n + 1) * D_MODEL].reshape(bsz, 1, D_MODEL) for n in range(6))
        if states is None:
            hp = jnp.zeros((bsz, POOL_HIST, POOL_WIDTH), F32)
            hs = jnp.zeros((bsz, 1, RW_IN), F32)
            s0 = jnp.zeros((bsz, RW_HEADS, RW_HD, RW_HD), F32)
            ck = cv = None
        else:
            ck, cv, hp, hs, s0 = (s[l] for s in states)
        x, new = _layer(x, mods, l, P, pos0, hp, hs, s0, ck, cv, chunks_per_tile)
        for lst, arr in zip(outs, new):
            lst.append(arr)
    return x, [jnp.stack(lst) for lst in outs]


def kernel(x_prompt, x_sample, c_prompt, c_sample, cache_attn_k, cache_attn_v, state_pool, state_rwkv_shift, state_rwkv_wkv, norm1_g, norm2_g, ada_w, ada_b, w_in, pool_w, pool_scale, q_norm_g, k_norm_g, attn_sink, rw_mu, rw_w0, rw_w2, rw_a0, rw_a2, rw_g2, rw_k_k, rw_k_a, rw_r_k, rw_ln_w, rw_ln_b, proj_pool, proj_attn, proj_rwkv, w_out, ffn_w1, ffn_w3, ffn_w2, router_w, router_b, moe_w1, moe_w3, moe_w2):
    W = dict(norm1_g=norm1_g, norm2_g=norm2_g, w_in=w_in, pool_w=pool_w,
             pool_scale=pool_scale, q_norm_g=q_norm_g, k_norm_g=k_norm_g, attn_sink=attn_sink,
             rw_mu=rw_mu, rw_w0=rw_w0, rw_w2=rw_w2, rw_a0=rw_a0, rw_a2=rw_a2, rw_g2=rw_g2,
             rw_k_k=rw_k_k, rw_k_a=rw_k_a, rw_r_k=rw_r_k, rw_ln_w=rw_ln_w, rw_ln_b=rw_ln_b,
             proj_pool=proj_pool, proj_attn=proj_attn, proj_rwkv=proj_rwkv, w_out=w_out,
             ffn_w1=ffn_w1, ffn_w3=ffn_w3, ffn_w2=ffn_w2, router_w=router_w, router_b=router_b,
             moe_w1=moe_w1, moe_w3=moe_w3, moe_w2=moe_w2)
    P = _prep_weights(W)
    n_prompt = c_prompt.shape[0]
    mod_all = _ada_mod(jnp.concatenate([c_prompt, c_sample], axis=0), ada_w, ada_b)
    y_prompt, p_new = _trunk(x_prompt, mod_all[:, :n_prompt], P, 0, None, RWKV_TILE_CHUNKS)
    y_sample, s_new = _trunk(x_sample, mod_all[:, n_prompt:], P, PAST_LEN,
                             (cache_attn_k, cache_attn_v, state_pool, state_rwkv_shift, state_rwkv_wkv), 1)
    return (y_prompt, y_sample, *p_new, *s_new)
```

```python
import functools

import jax
import jax.numpy as jnp
from jax import lax
from jax.experimental import pallas as pl
from jax.experimental.pallas import tpu as pltpu

F32 = jnp.float32
BF16 = jnp.bfloat16

D_MODEL = 1024
CHUNK = 64
NORM_EPS = 1e-6
N_BRANCH = 3
POOL_WINDOWS = (2, 4, 8, 16)
POOL_GROUP = 128
POOL_WIDTH = 512
POOL_HIST = 15
HEAD_DIM = 64
N_HEADS = 8
N_KV = 2
ATTN_WIDTH = 512
KV_WIDTH = 128
WINDOW = 128
ATTN_SCALE = HEAD_DIM ** -0.5
RW_HD = 64
RW_HEADS = 8
RW_WIDTH = 512
RW_DECAY_LORA = 64
RW_A_LORA = 64
RW_GATE_LORA = 160
RW_IN = 3 * RW_WIDTH + RW_DECAY_LORA + RW_A_LORA + RW_GATE_LORA
RW_GN_EPS = 64e-5
OFF_Q = POOL_WIDTH
OFF_K = OFF_Q + ATTN_WIDTH
OFF_V = OFF_K + KV_WIDTH
OFF_RW = OFF_V + KV_WIDTH
OFF_GATE = OFF_RW + RW_IN
N_EXPERTS = 8
PAST_LEN = 4096

LANES = 128
RW_PAD = 1920
RW_LORA_OFF = 3 * RW_WIDTH
RW_GATE_OFF = RW_LORA_OFF + LANES
GATE_COLS = N_BRANCH * D_MODEL
PAD_GATE = OFF_RW + RW_PAD
IN_COLS_PAD = PAD_GATE + GATE_COLS
HEADS_PER_PACK = 4
PACK = HEADS_PER_PACK * RW_HD
RWKV_TILE_CHUNKS = 4
RWKV_WAVE_UNITS = 8
ATTN_TILE_CHUNKS = 4
MOE_TILE = 1024
MOE_ROWS = 256
NEG_BIG = -1e30
VMEM_LIMIT = 56 * 1024 * 1024


def _cparams(sem):
    return pltpu.CompilerParams(dimension_semantics=sem, vmem_limit_bytes=VMEM_LIMIT)


def _dot(a, b):
    return jnp.dot(a.astype(BF16), b.astype(BF16), preferred_element_type=F32)


def _dot_nt(a, b):
    return lax.dot_general(a.astype(BF16), b.astype(BF16), (((1,), (1,)), ((), ())),
                           preferred_element_type=F32)


def _dot_tn(a, b):
    return lax.dot_general(a.astype(BF16), b.astype(BF16), (((0,), (0,)), ((), ())),
                           preferred_element_type=F32)


def _split_bf16(x):
    hi = x.astype(BF16)
    return hi, (x - hi.astype(F32)).astype(BF16)


def _sigmoid(x):
    return 1.0 / (1.0 + jnp.exp(-x))


def _group_ones(width, group):
    r = lax.broadcasted_iota(jnp.int32, (width, width), 0) // group
    c = lax.broadcasted_iota(jnp.int32, (width, width), 1) // group
    return (r == c).astype(BF16)


def _resident(a):
    return pl.BlockSpec(a.shape, lambda *_: (0,) * a.ndim, pipeline_mode=pl.Buffered(1))


def _tile_rows(bsz, seq, rows):
    if seq >= rows:
        assert seq % rows == 0
        return 1, rows
    bb = max(1, min(bsz, rows // seq))
    while bsz % bb:
        bb -= 1
    return bb, seq


def _mod_kernel(c_ref, w_ref, b_ref, o_ref):
    c = c_ref[...]
    o_ref[0] = _dot(c * _sigmoid(c), w_ref[0]) + b_ref[0]


def _ada_mod(c_all, ada_w, ada_b):
    depth, d, cols = ada_w.shape
    rows = c_all.shape[0]
    tn = 2048
    return pl.pallas_call(
        _mod_kernel,
        grid=(depth, cols // tn),
        in_specs=[pl.BlockSpec((rows, d), lambda l, j: (0, 0)),
                  pl.BlockSpec((1, d, tn), lambda l, j: (l, 0, j)),
                  pl.BlockSpec((1, 1, tn), lambda l, j: (l, 0, j))],
        out_specs=pl.BlockSpec((1, rows, tn), lambda l, j: (l, 0, j)),
        out_shape=jax.ShapeDtypeStruct((depth, rows, cols), F32),
        compiler_params=_cparams(("parallel", "parallel")),
        name="ada_mod",
    )(c_all, ada_w, ada_b.reshape(depth, 1, cols))


def _inproj_kernel(x_ref, sh_ref, sc_ref, g_ref, w_ref, qg_ref, kg_ref,
                   u_ref, q_ref, k_ref, v_ref, p_ref, gt_ref):
    bb, tt, d = x_ref.shape
    m = bb * tt
    x = x_ref[...]
    ms = jnp.mean(x * x, axis=-1, keepdims=True)
    h = x * lax.rsqrt(ms + NORM_EPS) * g_ref[...]
    h = h * (1.0 + sc_ref[...]) + sh_ref[...]
    hb = h.astype(BF16).reshape(m, d)

    u_ref[...] = jnp.dot(hb, w_ref[:, 0:OFF_Q], preferred_element_type=F32).reshape(bb, tt, POOL_WIDTH)

    zq = jnp.dot(hb, w_ref[:, OFF_Q:OFF_K], preferred_element_type=F32)
    msq = _dot(zq * zq, _group_ones(ATTN_WIDTH, HEAD_DIM)) * (1.0 / HEAD_DIM)
    qn = zq * lax.rsqrt(msq + NORM_EPS) * qg_ref[...] * ATTN_SCALE
    q_ref[...] = qn.astype(BF16).reshape(bb, tt, ATTN_WIDTH)

    zk = jnp.dot(hb, w_ref[:, OFF_K:OFF_V], preferred_element_type=F32)
    msk = _dot(zk * zk, _group_ones(KV_WIDTH, HEAD_DIM)) * (1.0 / HEAD_DIM)
    k_ref[...] = (zk * lax.rsqrt(msk + NORM_EPS) * kg_ref[...]).reshape(bb, tt, KV_WIDTH)

    v_ref[...] = jnp.dot(hb, w_ref[:, OFF_V:OFF_RW], preferred_element_type=F32).reshape(bb, tt, KV_WIDTH)

    p_ref[...] = jnp.dot(hb, w_ref[:, OFF_RW:PAD_GATE], preferred_element_type=F32).reshape(bb, tt, RW_PAD)

    zg = jnp.dot(hb, w_ref[:, PAD_GATE:IN_COLS_PAD], preferred_element_type=F32)
    gt_ref[...] = _sigmoid(zg).astype(BF16).reshape(bb, tt, GATE_COLS)


def _inproj(x, sh, sc, g1, w_in_p, qg, kg):
    bsz, seq, d = x.shape
    bb, tt = _tile_rows(bsz, seq, 512)
    tok = lambda f: pl.BlockSpec((bb, tt, f), lambda b, i: (b, i, 0))
    mod = pl.BlockSpec((bb, 1, d), lambda b, i: (b, 0, 0))
    full = _resident
    return pl.pallas_call(
        _inproj_kernel,
        grid=(bsz // bb, seq // tt),
        in_specs=[tok(d), mod, mod, full(g1), full(w_in_p), full(qg), full(kg)],
        out_specs=[tok(POOL_WIDTH), tok(ATTN_WIDTH), tok(KV_WIDTH), tok(KV_WIDTH), tok(RW_PAD), tok(GATE_COLS)],
        out_shape=[jax.ShapeDtypeStruct((bsz, seq, POOL_WIDTH), F32),
                   jax.ShapeDtypeStruct((bsz, seq, ATTN_WIDTH), BF16),
                   jax.ShapeDtypeStruct((bsz, seq, KV_WIDTH), F32),
                   jax.ShapeDtypeStruct((bsz, seq, KV_WIDTH), F32),
                   jax.ShapeDtypeStruct((bsz, seq, RW_PAD), F32),
                   jax.ShapeDtypeStruct((bsz, seq, GATE_COLS), BF16)],
        compiler_params=_cparams(("parallel", "parallel")),
        name="inproj",
    )(x, sh, sc, g1, w_in_p, qg, kg)


def _pool_kernel(u_ref, uprev_ref, hist_ref, pw_ref, ps_ref, o_ref, *, pos0):
    i = pl.program_id(1)
    tt = u_ref.shape[1]
    halo = uprev_ref.shape[1]
    u = u_ref[0]
    prev = jnp.where(i == 0, hist_ref[0], uprev_ref[0])
    ext = jnp.concatenate([prev, u], axis=0)
    sums = {}
    acc = ext
    for w in POOL_WINDOWS:
        acc = acc + pltpu.roll(acc, w // 2, axis=0)
        sums[w] = acc
    pos = pos0 + i * tt + lax.broadcasted_iota(jnp.int32, (tt, 1), 0)
    outs = []
    for g, w in enumerate(POOL_WINDOWS):
        sl = slice(g * POOL_GROUP, (g + 1) * POOL_GROUP)
        cnt = jnp.minimum(pos + 1, w).astype(F32)
        dlt = sums[w][halo:, sl] / cnt - u[:, sl]
        outs.append(_dot(dlt, pw_ref[g]))
    y = jnp.concatenate(outs, axis=1) * ps_ref[...]
    o_ref[0] = y.astype(BF16)


def _pool(u, hist16, pool_w, pool_scale, pos0):
    bsz, seq, _ = u.shape
    tt = min(seq, 512)
    halo = hist16.shape[1]
    per = tt // halo
    return pl.pallas_call(
        functools.partial(_pool_kernel, pos0=pos0),
        grid=(bsz, seq // tt),
        in_specs=[pl.BlockSpec((1, tt, POOL_WIDTH), lambda b, i: (b, i, 0)),
                  pl.BlockSpec((1, halo, POOL_WIDTH), lambda b, i: (b, jnp.maximum(i * per - 1, 0), 0)),
                  pl.BlockSpec((1, halo, POOL_WIDTH), lambda b, i: (b, 0, 0)),
                  pl.BlockSpec(pool_w.shape, lambda b, i: (0, 0, 0)),
                  pl.BlockSpec(pool_scale.shape, lambda b, i: (0, 0))],
        out_specs=pl.BlockSpec((1, tt, POOL_WIDTH), lambda b, i: (b, i, 0)),
        out_shape=jax.ShapeDtypeStruct((bsz, seq, POOL_WIDTH), BF16),
        compiler_params=_cparams(("parallel", "arbitrary")),
        name="pool",
    )(u, u, hist16, pool_w, pool_scale)


def _attn_core(q, kcat, vcat, bias, sink_ref):
    lo = lax.broadcasted_iota(jnp.int32, (1, KV_WIDTH), 1) < HEAD_DIM
    krot = pltpu.roll(kcat, HEAD_DIM, axis=1)
    vrot = pltpu.roll(vcat, HEAD_DIM, axis=1)
    khs, vhs, qps, sinks = [], [], [], []
    for g in range(N_KV):
        ksrc_lo, ksrc_hi = (kcat, krot) if g == 0 else (krot, kcat)
        vsrc_lo, vsrc_hi = (vcat, vrot) if g == 0 else (vrot, vcat)
        k_lo = jnp.where(lo, ksrc_lo, 0.0).astype(BF16)
        k_hi = jnp.where(lo, 0.0, ksrc_hi).astype(BF16)
        v_lo = jnp.where(lo, vsrc_lo, 0.0).astype(BF16)
        v_hi = jnp.where(lo, 0.0, vsrc_hi).astype(BF16)
        for jj in range(2):
            pair = 2 * g + jj
            qp = q[:, pair * LANES:(pair + 1) * LANES]
            for half, (kh, vh) in enumerate(((k_lo, v_lo), (k_hi, v_hi))):
                khs.append(kh)
                vhs.append(vh)
                qps.append(qp)
                sinks.append(sink_ref[2 * pair + half])
    ss = [_dot_nt(qp, kh) + bias for qp, kh in zip(qps, khs)]
    mxs = [jnp.maximum(jnp.max(s, axis=-1, keepdims=True), sk) for s, sk in zip(ss, sinks)]
    prs = [jnp.exp(s - mx) for s, mx in zip(ss, mxs)]
    inv = [1.0 / (jnp.sum(pr, axis=-1, keepdims=True) + jnp.exp(sk - mx))
           for pr, sk, mx in zip(prs, sinks, mxs)]
    ohs = [jnp.dot(pr.astype(BF16), vh, preferred_element_type=F32) * iv
           for pr, vh, iv in zip(prs, vhs, inv)]
    return jnp.concatenate([ohs[2 * p] + ohs[2 * p + 1] for p in range(N_HEADS // 2)], axis=1)


def _attn_banded_kernel(sink_ref, q_ref, kp_ref, kc_ref, vp_ref, vc_ref, band_ref, o_ref):
    i = pl.program_id(1)
    kcat = jnp.concatenate([kp_ref[0], kc_ref[0]], axis=0)
    vcat = jnp.concatenate([vp_ref[0], vc_ref[0]], axis=0)
    col = lax.broadcasted_iota(jnp.int32, (1, kcat.shape[0]), 1)
    first = jnp.where((col < WINDOW) & (i == 0), NEG_BIG, 0.0)
    o_ref[0] = _attn_core(q_ref[0], kcat, vcat, band_ref[...] + first, sink_ref).astype(BF16)


def _band_bias(tq):
    qc = jnp.arange(tq)[:, None] // CHUNK
    kc = jnp.arange(WINDOW + tq)[None, :] // CHUNK
    win = WINDOW // CHUNK
    return jnp.where((kc >= qc) & (kc <= qc + win), 0.0, NEG_BIG).astype(F32)


def _attn_banded(q, k, v, sink):
    bsz, seq, _ = q.shape
    tq = min(seq, ATTN_TILE_CHUNKS * CHUNK)
    assert seq % tq == 0 and tq % WINDOW == 0
    per = tq // WINDOW
    band = _band_bias(tq)
    prev = pl.BlockSpec((1, WINDOW, KV_WIDTH), lambda b, i: (b, jnp.maximum(i * per - 1, 0), 0))
    cur = pl.BlockSpec((1, tq, KV_WIDTH), lambda b, i: (b, i, 0))
    return pl.pallas_call(
        _attn_banded_kernel,
        grid=(bsz, seq // tq),
        in_specs=[pl.BlockSpec(memory_space=pltpu.SMEM),
                  pl.BlockSpec((1, tq, ATTN_WIDTH), lambda b, i: (b, i, 0)),
                  prev, cur, prev, cur,
                  pl.BlockSpec(band.shape, lambda b, i: (0, 0))],
        out_specs=pl.BlockSpec((1, tq, ATTN_WIDTH), lambda b, i: (b, i, 0)),
        out_shape=jax.ShapeDtypeStruct((bsz, seq, ATTN_WIDTH), BF16),
        compiler_params=_cparams(("parallel", "parallel")),
        name="attn_banded",
    )(sink, q, k, k, v, v, band)


def _attn_cached_kernel(sink_ref, q_ref, ck_ref, k_ref, cv_ref, v_ref, o_ref):
    cache, new = ck_ref.shape[1], k_ref.shape[1]
    span = 3 * CHUNK
    pad = jnp.zeros((span - cache - new, KV_WIDTH), F32)
    kcat = jnp.concatenate([ck_ref[0], k_ref[0], pad], axis=0)
    vcat = jnp.concatenate([cv_ref[0], v_ref[0], pad], axis=0)
    col = lax.broadcasted_iota(jnp.int32, (1, span), 1)
    bias = jnp.where(col < cache + new, 0.0, NEG_BIG)
    o_ref[0] = _attn_core(q_ref[0], kcat, vcat, bias, sink_ref).astype(BF16)


def _attn_cached(q, k, v, ck, cv, sink):
    bsz, seq, _ = q.shape
    cache = ck.shape[1]
    assert cache + seq <= 3 * CHUNK and (3 * CHUNK - cache - seq) % 8 == 0
    blk = lambda t, f: pl.BlockSpec((1, t, f), lambda b: (b, 0, 0))
    return pl.pallas_call(
        _attn_cached_kernel,
        grid=(bsz,),
        in_specs=[pl.BlockSpec(memory_space=pltpu.SMEM), blk(seq, ATTN_WIDTH),
                  blk(cache, KV_WIDTH), blk(seq, KV_WIDTH), blk(cache, KV_WIDTH), blk(seq, KV_WIDTH)],
        out_specs=blk(seq, ATTN_WIDTH),
        out_shape=jax.ShapeDtypeStruct((bsz, seq, ATTN_WIDTH), BF16),
        compiler_params=_cparams(("parallel",)),
        name="attn_cached",
    )(sink, q, ck, k, cv, v)


def _rwkv_kernel(p_ref, pprev_ref, hist_ref, s0_ref, mu_ref, vec_ref, wwa_ref, g2_ref,
                 ones_ref, mbd_ref, mbd2_ref, tri_ref, mst_ref,
                 o_ref, sout_ref, s_scr, *, chunk):
    i = pl.program_id(1)
    tt = p_ref.shape[1]
    c_len = chunk
    n_chunks = tt // c_len
    hp = HEADS_PER_PACK
    wide = hp * c_len
    n_pack = RW_WIDTH // PACK

    @pl.when(i == 0)
    def _():
        s_scr[...] = s0_ref[0]

    p = p_ref[0]
    prev_row = jnp.where(i == 0, hist_ref[0], pprev_ref[0, 7:8, :])
    row = lax.broadcasted_iota(jnp.int32, (tt, 1), 0)
    prev = jnp.where(row == 0, prev_row, pltpu.roll(p, 1, axis=0))
    pm = p + (prev - p) * mu_ref[...]

    w0, a0, k_k, k_a = vec_ref[0:1], vec_ref[1:2], vec_ref[2:3], vec_ref[3:4]
    r_k, ln_w, ln_b = vec_ref[4:5], vec_ref[5:6], vec_ref[6:7]
    r = pm[:, 0:RW_WIDTH]
    k = pm[:, RW_WIDTH:2 * RW_WIDTH]
    v = pm[:, 2 * RW_WIDTH:3 * RW_WIDTH]
    xa = pm[:, RW_LORA_OFF:RW_LORA_OFF + LANES]
    lane = lax.broadcasted_iota(jnp.int32, (1, LANES), 1)
    xa = jnp.where(lane < RW_DECAY_LORA, jnp.tanh(xa), xa)
    la = _dot(xa, wwa_ref[...])
    wlin = -(w0 + la[:, :RW_WIDTH])
    softplus = jnp.maximum(wlin, 0.0) + jnp.log(1.0 + jnp.exp(-jnp.abs(wlin)))
    lw = -jnp.exp(-softplus - 0.5)
    a = _sigmoid(a0 + la[:, RW_WIDTH:])
    g = _dot(_sigmoid(pm[:, RW_GATE_OFF:RW_PAD]), g2_ref[...])
    ones_g = ones_ref[...]
    kk = k * k_k
    kk = kk / jnp.maximum(jnp.sqrt(_dot(kk * kk, ones_g)), 1e-12)
    k2 = k * (1.0 + (a - 1.0) * k_a)
    bonus = _dot(r * k2 * r_k, ones_g) * v
    kka = kk * a

    in_chunk = row % c_len
    cs = lw
    sft = 1
    while sft < c_len:
        cs = cs + jnp.where(in_chunk >= sft, pltpu.roll(cs, sft, axis=0), 0.0)
        sft *= 2
    tot = jnp.concatenate(
        [jnp.broadcast_to(cs[(c + 1) * c_len - 1:(c + 1) * c_len, :], (c_len, RW_WIDTH)) for c in range(n_chunks)],
        axis=0)
    p_inv = jnp.exp(-cs)
    p_end = jnp.exp(tot - cs)
    at = -kk * jnp.exp(cs - lw)
    bt = kka * p_inv
    kt = k2 * p_inv
    rt = r * jnp.exp(cs)
    bb_ = kka * p_end
    kb = k2 * p_end

    mbd, mbd2 = mbd_ref[...], mbd2_ref[...]
    strict, lower, eye_p = tri_ref[0], tri_ref[1], tri_ref[2]
    mask_state, eye_state = mst_ref[0], mst_ref[1]
    bd = lambda x: jnp.concatenate([x.astype(BF16)] * hp, axis=0) * mbd
    bd2 = lambda x: jnp.concatenate([x.astype(BF16)] * hp, axis=0) * mbd2
    zeros_cp = jnp.zeros((c_len, PACK), F32)

    view = lambda x, u: x[u[0] * c_len:(u[0] + 1) * c_len, u[1] * PACK:(u[1] + 1) * PACK]

    def prepass(units):
        s1 = [_dot_nt(jnp.concatenate([view(at, u), view(rt, u)], axis=0),
                      jnp.concatenate([bd(view(bt, u)), bd(view(kt, u))], axis=0)) for u in units]
        l_ab = [s[:c_len, :wide] * strict for s in s1]
        l_ak = [s[:c_len, wide:] * strict for s in s1]
        g_bk = [jnp.concatenate([s[c_len:, :wide] * lower, s[c_len:, wide:] * lower], axis=1) for s in s1]
        pw = [_dot(l, bd2(l)) for l in l_ab]
        tm = [eye_p + l for l in l_ab]
        span = 2
        while span < c_len:
            rhs = [bd2(x) for x in pw]
            if 2 * span < c_len:
                both = [_dot(jnp.concatenate([x, t], axis=0), rr) for x, t, rr in zip(pw, tm, rhs)]
                pw = [bo[:c_len] for bo in both]
                tm = [t + bo[c_len:] for t, bo in zip(tm, both)]
            else:
                tm = [t + _dot(t, rr) for t, rr in zip(tm, rhs)]
            span *= 2
        bdv = [bd(view(v, u)) for u in units]
        x1 = [_dot(l, bv) for l, bv in zip(l_ak, bdv)]
        wu = [_dot(t, jnp.concatenate([bd(view(at, u)), bd(x)], axis=1)) for t, u, x in zip(tm, units, x1)]
        gw = [_dot(gg[:, :wide], jnp.concatenate([bd(w[:, :PACK]), bd(w[:, PACK:])], axis=1))
              for gg, w in zip(g_bk, wu)]
        gv = [_dot(gg[:, wide:], bv) for gg, bv in zip(g_bk, bdv)]
        gy = [jnp.concatenate([a_[:, :PACK], a_[:, PACK:] + b_], axis=1) for a_, b_ in zip(gw, gv)]
        mn = [_dot_tn(jnp.concatenate([view(bb_, u), view(kb, u)], axis=0),
                      jnp.concatenate([w, jnp.concatenate([zeros_cp, view(v, u)], axis=1)], axis=0))
              for u, w in zip(units, wu)]
        return gy, mn

    def advance(c, gy, mn, states):
        y_halves, new_states = [], []
        for j in range(n_pack):
            u = (c, j)
            p_tot = jnp.exp(tot[c * c_len:c * c_len + 1, j * PACK:(j + 1) * PACK])
            rh = view(rt, u) + gy[j][:, :PACK]
            mc = mn[j][:, :PACK] * mask_state + eye_state * p_tot
            mc_hi, mc_lo = _split_bf16(mc)
            st_hi, st_lo = _split_bf16(states[j])
            main = jnp.dot(jnp.concatenate([rh.astype(BF16), mc_hi, mc_lo], axis=0), st_hi,
                           preferred_element_type=F32)
            y_halves.append(main[:c_len] + gy[j][:, PACK:])
            new_states.append(main[c_len:c_len + PACK] + main[c_len + PACK:]
                              + jnp.dot(mc_hi, st_lo, preferred_element_type=F32)
                              + mn[j][:, PACK:] * mask_state)
        return jnp.concatenate(y_halves, axis=1), new_states

    states = [s_scr[j] for j in range(n_pack)]
    y_rows = []
    wave_chunks = max(1, RWKV_WAVE_UNITS // n_pack)
    for c0 in range(0, n_chunks, wave_chunks):
        chunks = range(c0, min(c0 + wave_chunks, n_chunks))
        gy_w, mn_w = prepass([(c, j) for c in chunks for j in range(n_pack)])
        for n, c in enumerate(chunks):
            y_c, states = advance(c, gy_w[n * n_pack:(n + 1) * n_pack], mn_w[n * n_pack:(n + 1) * n_pack], states)
            y_rows.append(y_c)
    for j in range(n_pack):
        s_scr[j] = states[j]
    y = y_rows[0] if n_chunks == 1 else jnp.concatenate(y_rows, axis=0)

    inv_hd = 1.0 / RW_HD
    dy = y - _dot(y, ones_g) * inv_hd
    var = _dot(dy * dy, ones_g) * inv_hd
    yn = dy * lax.rsqrt(var + RW_GN_EPS) * ln_w + ln_b
    o_ref[0] = ((yn + bonus) * g).astype(BF16)

    @pl.when(i == pl.num_programs(1) - 1)
    def _():
        sout_ref[0] = s_scr[...]


def _rwkv_masks(c_len):
    wide = HEADS_PER_PACK * c_len
    r_blk = jnp.arange(wide)[:, None] // c_len
    mbd = (r_blk == jnp.arange(PACK)[None, :] // RW_HD).astype(BF16)
    mbd2 = (r_blk == jnp.arange(wide)[None, :] // c_len).astype(BF16)
    t_idx = jnp.arange(c_len)[:, None]
    s_idx = jnp.arange(wide)[None, :] % c_len
    tri = jnp.stack([s_idx < t_idx, s_idx <= t_idx, s_idx == t_idx]).astype(F32)
    st = jnp.arange(PACK)
    mst = jnp.stack([(st[:, None] // RW_HD) == (st[None, :] // RW_HD), st[:, None] == st[None, :]]).astype(F32)
    g_idx = jnp.arange(RW_WIDTH) // RW_HD
    ones_g = (g_idx[:, None] == g_idx[None, :]).astype(BF16)
    return ones_g, mbd, mbd2, tri, mst


def _rwkv(p, hist, s0_packed, mu, vec, wwa, g2p, chunks_per_tile):
    bsz, seq, _ = p.shape
    chunk = min(CHUNK, seq)
    tt = min(seq, chunk * chunks_per_tile)
    per = tt // 8
    npk = RW_WIDTH // PACK
    consts = _rwkv_masks(chunk)
    full = _resident
    return pl.pallas_call(
        functools.partial(_rwkv_kernel, chunk=chunk),
        grid=(bsz, seq // tt),
        in_specs=[pl.BlockSpec((1, tt, RW_PAD), lambda b, i: (b, i, 0)),
                  pl.BlockSpec((1, 8, RW_PAD), lambda b, i: (b, jnp.maximum(i * per - 1, 0), 0)),
                  pl.BlockSpec((1, 1, RW_PAD), lambda b, i: (b, 0, 0)),
                  pl.BlockSpec((1, npk, PACK, PACK), lambda b, i: (b, 0, 0, 0)),
                  full(mu), full(vec), full(wwa), full(g2p)] + [full(c) for c in consts],
        out_specs=[pl.BlockSpec((1, tt, RW_WIDTH), lambda b, i: (b, i, 0)),
                   pl.BlockSpec((1, npk, PACK, PACK), lambda b, i: (b, 0, 0, 0))],
        out_shape=[jax.ShapeDtypeStruct((bsz, seq, RW_WIDTH), BF16),
                   jax.ShapeDtypeStruct((bsz, npk, PACK, PACK), F32)],
        scratch_shapes=[pltpu.VMEM((npk, PACK, PACK), F32)],
        compiler_params=_cparams(("parallel", "arbitrary")),
        name="rwkv",
    )(p, p, hist, s0_packed, mu, vec, wwa, g2p, *consts)


def _pack_state(wkv):
    bsz = wkv.shape[0]
    npk = RW_HEADS // HEADS_PER_PACK
    st = jnp.swapaxes(wkv, -1, -2).reshape(bsz, npk, HEADS_PER_PACK, RW_HD, RW_HD)
    eye = jnp.eye(HEADS_PER_PACK, dtype=wkv.dtype)
    return jnp.einsum('bjhkv,hg->bjhkgv', st, eye).reshape(bsz, npk, PACK, PACK)


def _unpack_state(sp):
    bsz, npk = sp.shape[:2]
    s6 = sp.reshape(bsz, npk, HEADS_PER_PACK, RW_HD, HEADS_PER_PACK, RW_HD)
    blocks = jnp.stack([s6[:, :, h, :, h, :] for h in range(HEADS_PER_PACK)], axis=2)
    return jnp.swapaxes(blocks.reshape(bsz, RW_HEADS, RW_HD, RW_HD), -1, -2)


def _merge_kernel(x_ref, op_ref, oa_ref, or_ref, gt_ref, ga_ref, sh_ref, sc_ref, g2_ref,
                  pp_ref, pa_ref, pr_ref, wo_ref, x1_ref, h2_ref):
    bb, tt, d = x_ref.shape
    m = bb * tt
    gates = gt_ref[...].reshape(m, GATE_COLS)
    merged = None
    for n, (o_ref, w_ref) in enumerate(((op_ref, pp_ref), (oa_ref, pa_ref), (or_ref, pr_ref))):
        o = o_ref[...].reshape(m, o_ref.shape[2])
        term = gates[:, n * d:(n + 1) * d].astype(F32) * jnp.dot(o, w_ref[...], preferred_element_type=F32)
        merged = term if merged is None else merged + term
    y = _dot(merged, wo_ref[...]).reshape(bb, tt, d)
    x1 = x_ref[...] + ga_ref[...] * y
    x1_ref[...] = x1
    ms = jnp.mean(x1 * x1, axis=-1, keepdims=True)
    h = x1 * lax.rsqrt(ms + NORM_EPS) * g2_ref[...]
    h2_ref[...] = (h * (1.0 + sc_ref[...]) + sh_ref[...]).astype(BF16)


def _merge(x, o_pool, o_attn, o_rw, gates, ga1, sh2, sc2, g2, pp, pa, pr, wo):
    bsz, seq, d = x.shape
    bb, tt = _tile_rows(bsz, seq, 512)
    tok = lambda f: pl.BlockSpec((bb, tt, f), lambda b, i: (b, i, 0))
    mod = pl.BlockSpec((bb, 1, d), lambda b, i: (b, 0, 0))
    full = _resident
    return pl.pallas_call(
        _merge_kernel,
        grid=(bsz // bb, seq // tt),
        in_specs=[tok(d), tok(POOL_WIDTH), tok(ATTN_WIDTH), tok(RW_WIDTH), tok(GATE_COLS),
                  mod, mod, mod, full(g2), full(pp), full(pa), full(pr), full(wo)],
        out_specs=[tok(d), tok(d)],
        out_shape=[jax.ShapeDtypeStruct((bsz, seq, d), F32), jax.ShapeDtypeStruct((bsz, seq, d), BF16)],
        compiler_params=_cparams(("parallel", "parallel")),
        name="merge",
    )(x, o_pool, o_attn, o_rw, gates, ga1, sh2, sc2, g2, pp, pa, pr, wo)


def _swiglu_rows(h, w1, w3, w2):
    a = jnp.dot(h, w1, preferred_element_type=F32)
    b = jnp.dot(h, w3, preferred_element_type=F32)
    return jnp.dot((a * _sigmoid(a) * b).astype(BF16), w2, preferred_element_type=F32)


def _ffn_kernel(x_ref, h_ref, ga_ref, w1_ref, w3_ref, w2_ref, o_ref, *, n_split):
    bb, tt, d = x_ref.shape
    h = h_ref[...].reshape(bb * tt, d)
    ff = w1_ref.shape[1]
    step = ff // n_split
    f = None
    for s in range(n_split):
        sl = slice(s * step, (s + 1) * step)
        part = _swiglu_rows(h, w1_ref[:, sl], w3_ref[:, sl], w2_ref[sl, :])
        f = part if f is None else f + part
    o_ref[...] = x_ref[...] + ga_ref[...] * f.reshape(bb, tt, d)


def _ffn(x, h, ga, w1, w3, w2):
    bsz, seq, d = x.shape
    bb, tt = _tile_rows(bsz, seq, 512)
    tok = pl.BlockSpec((bb, tt, d), lambda b, i: (b, i, 0))
    mod = pl.BlockSpec((bb, 1, d), lambda b, i: (b, 0, 0))
    full = _resident
    return pl.pallas_call(
        functools.partial(_ffn_kernel, n_split=2),
        grid=(bsz // bb, seq // tt),
        in_specs=[tok, tok, mod, full(w1), full(w3), full(w2)],
        out_specs=tok,
        out_shape=jax.ShapeDtypeStruct((bsz, seq, d), F32),
        compiler_params=_cparams(("parallel", "parallel")),
        name="ffn",
    )(x, h, ga, w1, w3, w2)


def _router_kernel(h_ref, rw_ref, rb_ref, tri_ref, gate_ref, rank_ref, cnt_ref):
    bb, tt, d = h_ref.shape
    m = bb * tt
    h = h_ref[...].reshape(m, d)
    lane = lax.broadcasted_iota(jnp.int32, (m, LANES), 1)
    logits = _dot(h, rw_ref[...]) + rb_ref[...]
    logits = jnp.where(lane < N_EXPERTS, logits, NEG_BIG)
    mx = jnp.max(logits, axis=-1, keepdims=True)
    ex = jnp.exp(logits - mx)
    probs = ex / jnp.sum(ex, axis=-1, keepdims=True)
    p1 = jnp.max(probs, axis=-1, keepdims=True)
    i1 = jnp.min(jnp.where(probs == p1, lane, LANES), axis=-1, keepdims=True)
    rest = jnp.where(lane == i1, -1.0, probs)
    p2 = jnp.max(rest, axis=-1, keepdims=True)
    i2 = jnp.min(jnp.where(rest == p2, lane, LANES), axis=-1, keepdims=True)
    tot = p1 + p2
    gate = jnp.where(lane == i1, p1 / tot, 0.0) + jnp.where(lane == i2, p2 / tot, 0.0)
    gate_ref[...] = gate.reshape(bb, tt, LANES)
    assigned = jnp.where((lane == i1) | (lane == i2), 1.0, 0.0)
    rr = _dot_tn(assigned, tri_ref[...])
    rank = jnp.where(rr[:, m:] > 0.5, rr[:, :m], -1.0)
    rank_ref[0] = rank[:N_EXPERTS].astype(jnp.int32)
    cnt = jnp.sum(assigned, axis=0, keepdims=True)
    cnt_ref[0] = jnp.broadcast_to(cnt, (8, LANES)).astype(jnp.int32)


def _moe_kernel(cnt_ref, x_ref, h_ref, ga_ref, gate_ref, rank_ref, w1_ref, w3_ref, w2_ref, o_ref):
    e = pl.program_id(2)
    tile = pl.program_id(0) * pl.num_programs(1) + pl.program_id(1)
    bb, tt, d = x_ref.shape
    m = bb * tt

    @pl.when(e == 0)
    def _():
        o_ref[...] = x_ref[...]

    count = cnt_ref[tile * N_EXPERTS + e]
    h = h_ref[...].reshape(m, d)
    lane = lax.broadcasted_iota(jnp.int32, (m, LANES), 1)
    gate = jnp.sum(jnp.where(lane == e, gate_ref[...].reshape(m, LANES), 0.0), axis=-1, keepdims=True)
    rank_row = rank_ref[0, pl.ds(e, 1), :]

    def expert_rows(start, rows):
        row = lax.broadcasted_iota(jnp.int32, (rows, m), 0)
        sel = jnp.where(rank_row == row + start, 1.0, 0.0).astype(BF16)
        xs = jnp.dot(sel, h, preferred_element_type=F32).astype(BF16)
        f = _swiglu_rows(xs, w1_ref[0], w3_ref[0], w2_ref[0])
        o_ref[...] += ga_ref[...] * (gate * _dot_tn(sel, f)).reshape(bb, tt, d)

    half = MOE_ROWS // 2
    rem = count % MOE_ROWS
    n_full = count // MOE_ROWS + jnp.where(rem > half, 1, 0)

    def full_block(kb, carry):
        expert_rows(kb * MOE_ROWS, MOE_ROWS)
        return carry

    lax.fori_loop(0, n_full, full_block, 0)

    @pl.when((rem > 0) & (rem <= half))
    def _():
        expert_rows(n_full * MOE_ROWS, half)


def _moe(x, h, ga, rw_p, rb_p, w1, w3, w2):
    bsz, seq, d = x.shape
    bb, tt = _tile_rows(bsz, seq, MOE_TILE)
    m = bb * tt
    nb, nt = bsz // bb, seq // tt
    n_e, _, ffe = w1.shape
    t_i = jnp.arange(m)
    tri = jnp.concatenate([t_i[:, None] < t_i[None, :], t_i[:, None] == t_i[None, :]], axis=1).astype(BF16)
    gate, rank, cnt = pl.pallas_call(
        _router_kernel,
        grid=(nb, nt),
        in_specs=[pl.BlockSpec((bb, tt, d), lambda b, i: (b, i, 0)),
                  pl.BlockSpec(rw_p.shape, lambda b, i: (0, 0)),
                  pl.BlockSpec(rb_p.shape, lambda b, i: (0, 0)),
                  pl.BlockSpec(tri.shape, lambda b, i: (0, 0))],
        out_specs=[pl.BlockSpec((bb, tt, LANES), lambda b, i: (b, i, 0)),
                   pl.BlockSpec((1, N_EXPERTS, m), lambda b, i: (b * nt + i, 0, 0)),
                   pl.BlockSpec((1, 8, LANES), lambda b, i: (b * nt + i, 0, 0))],
        out_shape=[jax.ShapeDtypeStruct((bsz, seq, LANES), F32),
                   jax.ShapeDtypeStruct((nb * nt, N_EXPERTS, m), jnp.int32),
                   jax.ShapeDtypeStruct((nb * nt, 8, LANES), jnp.int32)],
        compiler_params=_cparams(("parallel", "parallel")),
        name="router",
    )(h, rw_p, rb_p, tri)
    counts = cnt[:, 0, :N_EXPERTS].reshape(-1)
    tok = lambda f: pl.BlockSpec((bb, tt, f), lambda b, i, e, c: (b, i, 0))
    return pl.pallas_call(
        _moe_kernel,
        grid_spec=pltpu.PrefetchScalarGridSpec(
            num_scalar_prefetch=1,
            grid=(nb, nt, n_e),
            in_specs=[tok(d), tok(d), pl.BlockSpec((bb, 1, d), lambda b, i, e, c: (b, 0, 0)), tok(LANES),
                      pl.BlockSpec((1, N_EXPERTS, m), lambda b, i, e, c: (b * nt + i, 0, 0)),
                      pl.BlockSpec((1, d, ffe), lambda b, i, e, c: (e, 0, 0)),
                      pl.BlockSpec((1, d, ffe), lambda b, i, e, c: (e, 0, 0)),
                      pl.BlockSpec((1, ffe, d), lambda b, i, e, c: (e, 0, 0))],
            out_specs=tok(d)),
        out_shape=jax.ShapeDtypeStruct((bsz, seq, d), F32),
        compiler_params=_cparams(("parallel", "parallel", "arbitrary")),
        name="moe",
    )(counts, x, h, ga, gate, rank, w1, w3, w2)


def _prep_weights(W):
    depth = W['w_in'].shape[0]
    P = {}
    w_in = W['w_in']
    pad = jnp.zeros((depth, D_MODEL, RW_PAD - RW_IN), w_in.dtype)
    P['w_in'] = jnp.concatenate([w_in[:, :, :OFF_GATE], pad, w_in[:, :, OFF_GATE:]], axis=-1).astype(BF16)
    P['qg'] = jnp.tile(W['q_norm_g'], (1, N_HEADS)).reshape(depth, 1, ATTN_WIDTH)
    P['kg'] = jnp.tile(W['k_norm_g'], (1, N_KV)).reshape(depth, 1, KV_WIDTH)
    P['g1'] = W['norm1_g'].reshape(depth, 1, D_MODEL)
    P['g2'] = W['norm2_g'].reshape(depth, 1, D_MODEL)
    P['pool_w'] = W['pool_w'].astype(BF16)
    P['pool_scale'] = W['pool_scale'].reshape(depth, 1, POOL_WIDTH)
    P['mu'] = jnp.pad(W['rw_mu'], ((0, 0), (0, RW_PAD - RW_IN))).reshape(depth, 1, RW_PAD)
    vec = jnp.stack([W['rw_w0'], W['rw_a0'], W['rw_k_k'], W['rw_k_a'],
                     W['rw_r_k'].reshape(depth, RW_WIDTH), W['rw_ln_w'], W['rw_ln_b'],
                     jnp.zeros_like(W['rw_w0'])], axis=1)
    P['vec'] = vec
    zl = jnp.zeros_like(W['rw_w2'])
    P['wwa'] = jnp.concatenate([jnp.concatenate([W['rw_w2'], zl], axis=2),
                                jnp.concatenate([zl, W['rw_a2']], axis=2)], axis=1).astype(BF16)
    P['g2p'] = jnp.pad(W['rw_g2'], ((0, 0), (0, RW_PAD - RW_GATE_OFF - RW_GATE_LORA), (0, 0))).astype(BF16)
    for name in ('proj_pool', 'proj_attn', 'proj_rwkv', 'w_out', 'ffn_w1', 'ffn_w3', 'ffn_w2',
                 'moe_w1', 'moe_w3', 'moe_w2'):
        P[name] = W[name].astype(BF16)
    P['router_w'] = jnp.pad(W['router_w'], ((0, 0), (0, 0), (0, LANES - N_EXPERTS))).astype(BF16)
    P['router_b'] = jnp.pad(W['router_b'], ((0, 0), (0, LANES - N_EXPERTS))).reshape(-1, 1, LANES)
    P['sink'] = W['attn_sink']
    return P


def _layer(x, mods, l, P, pos0, hist_pool, hist_shift, wkv0, ck, cv, chunks_per_tile):
    bsz, seq, _ = x.shape
    sh1, sc1, ga1, sh2, sc2, ga2 = mods
    u, q, k, v, p, gates = _inproj(x, sh1, sc1, P['g1'][l], P['w_in'][l], P['qg'][l], P['kg'][l])
    hist16 = jnp.pad(hist_pool, ((0, 0), (1, 0), (0, 0)))
    o_pool = _pool(u, hist16, P['pool_w'][l], P['pool_scale'][l], pos0)
    if ck is None:
        o_attn = _attn_banded(q, k, v, P['sink'][l])
        new_k, new_v = k[:, -WINDOW:], v[:, -WINDOW:]
    else:
        o_attn = _attn_cached(q, k, v, ck.reshape(bsz, -1, KV_WIDTH), cv.reshape(bsz, -1, KV_WIDTH), P['sink'][l])
        new_k, new_v = k, v
    hist_p = jnp.pad(hist_shift, ((0, 0), (0, 0), (0, RW_PAD - RW_IN)))
    o_rw, s_out = _rwkv(p, hist_p, _pack_state(wkv0), P['mu'][l], P['vec'][l], P['wwa'][l], P['g2p'][l],
                        chunks_per_tile)
    x1, h2 = _merge(x, o_pool, o_attn, o_rw, gates, ga1, sh2, sc2, P['g2'][l],
                    P['proj_pool'][l], P['proj_attn'][l], P['proj_rwkv'][l], P['w_out'][l])
    i = l // 2
    if l % 2 == 0:
        x2 = _ffn(x1, h2, ga2, P['ffn_w1'][i], P['ffn_w3'][i], P['ffn_w2'][i])
    else:
        x2 = _moe(x1, h2, ga2, P['router_w'][i], P['router_b'][i], P['moe_w1'][i], P['moe_w3'][i], P['moe_w2'][i])
    new_pool = jnp.concatenate([hist_pool, u], axis=1)[:, -POOL_HIST:]
    new = (new_k.reshape(bsz, -1, N_KV, HEAD_DIM), new_v.reshape(bsz, -1, N_KV, HEAD_DIM),
           new_pool, p[:, -1:, :RW_IN], _unpack_state(s_out))
    return x2, new


def _trunk(x, mod_all, P, pos0, states, chunks_per_tile):
    bsz = x.shape[0]
    depth = P['w_in'].shape[0]
    outs = ([], [], [], [], [])
    for l in range(depth):
        mods = tuple(mod_all[l, :, n * D_MODEL:(n + 1) * D_MODEL].reshape(bsz, 1, D_MODEL) for n in range(6))
        if states is None:
            hp = jnp.zeros((bsz, POOL_HIST, POOL_WIDTH), F32)
            hs = jnp.zeros((bsz, 1, RW_IN), F32)
            s0 = jnp.zeros((bsz, RW_HEADS, RW_HD, RW_HD), F32)
            ck = cv = None
        else:
            ck, cv, hp, hs, s0 = (s[l] for s in states)
        x, new = _layer(x, mods, l, P, pos0, hp, hs, s0, ck, cv, chunks_per_tile)
        for lst, arr in zip(outs, new):
            lst.append(arr)
    return x, [jnp.stack(lst) for lst in outs]


def kernel(x_prompt, x_sample, c_prompt, c_sample, cache_attn_k, cache_attn_v, state_pool, state_rwkv_shift, state_rwkv_wkv, norm1_g, norm2_g, ada_w, ada_b, w_in, pool_w, pool_scale, q_norm_g, k_norm_g, attn_sink, rw_mu, rw_w0, rw_w2, rw_a0, rw_a2, rw_g2, rw_k_k, rw_k_a, rw_r_k, rw_ln_w, rw_ln_b, proj_pool, proj_attn, proj_rwkv, w_out, ffn_w1, ffn_w3, ffn_w2, router_w, router_b, moe_w1, moe_w3, moe_w2):
    W = dict(norm1_g=norm1_g, norm2_g=norm2_g, w_in=w_in, pool_w=pool_w,
             pool_scale=pool_scale, q_norm_g=q_norm_g, k_norm_g=k_norm_g, attn_sink=attn_sink,
             rw_mu=rw_mu, rw_w0=rw_w0, rw_w2=rw_w2, rw_a0=rw_a0, rw_a2=rw_a2, rw_g2=rw_g2,
             rw_k_k=rw_k_k, rw_k_a=rw_k_a, rw_r_k=rw_r_k, rw_ln_w=rw_ln_w, rw_ln_b=rw_ln_b,
             proj_pool=proj_pool, proj_attn=proj_attn, proj_rwkv=proj_rwkv, w_out=w_out,
             ffn_w1=ffn_w1, ffn_w3=ffn_w3, ffn_w2=ffn_w2, router_w=router_w, router_b=router_b,
             moe_w1=moe_w1, moe_w3=moe_w3, moe_w2=moe_w2)
    P = _prep_weights(W)
    n_prompt = c_prompt.shape[0]
    mod_all = _ada_mod(jnp.concatenate([c_prompt, c_sample], axis=0), ada_w, ada_b)
    y_prompt, p_new = _trunk(x_prompt, mod_all[:, :n_prompt], P, 0, None, RWKV_TILE_CHUNKS)
    y_sample, s_new = _trunk(x_sample, mod_all[:, n_prompt:], P, PAST_LEN,
                             (cache_attn_k, cache_attn_v, state_pool, state_rwkv_shift, state_rwkv_wkv), 1)
    return (y_prompt, y_sample, *p_new, *s_new)
```

```python
import functools

import jax
import jax.numpy as jnp
from jax import lax
from jax.experimental import pallas as pl
from jax.experimental.pallas import tpu as pltpu

F32 = jnp.float32
BF16 = jnp.bfloat16

D_MODEL = 1024
CHUNK = 64
NORM_EPS = 1e-6
N_BRANCH = 3
POOL_WINDOWS = (2, 4, 8, 16)
POOL_GROUP = 128
POOL_WIDTH = 512
POOL_HIST = 15
HEAD_DIM = 64
N_HEADS = 8
N_KV = 2
ATTN_WIDTH = 512
KV_WIDTH = 128
WINDOW = 128
ATTN_SCALE = HEAD_DIM ** -0.5
RW_HD = 64
RW_HEADS = 8
RW_WIDTH = 512
RW_DECAY_LORA = 64
RW_A_LORA = 64
RW_GATE_LORA = 160
RW_IN = 3 * RW_WIDTH + RW_DECAY_LORA + RW_A_LORA + RW_GATE_LORA
RW_GN_EPS = 64e-5
OFF_Q = POOL_WIDTH
OFF_K = OFF_Q + ATTN_WIDTH
OFF_V = OFF_K + KV_WIDTH
OFF_RW = OFF_V + KV_WIDTH
OFF_GATE = OFF_RW + RW_IN
N_EXPERTS = 8
PAST_LEN = 4096

LANES = 128
RW_PAD = 1920
RW_LORA_OFF = 3 * RW_WIDTH
RW_GATE_OFF = RW_LORA_OFF + LANES
GATE_COLS = N_BRANCH * D_MODEL
PAD_GATE = OFF_RW + RW_PAD
IN_COLS_PAD = PAD_GATE + GATE_COLS
HEADS_PER_PACK = 4
PACK = HEADS_PER_PACK * RW_HD
RWKV_TILE_CHUNKS = 8
RWKV_WAVES = 2
ATTN_TILE_CHUNKS = 8
MOE_TILE = 1024
MOE_ROWS = 256
NEG_BIG = -1e30
VMEM_LIMIT = 56 * 1024 * 1024


def _cparams(sem):
    return pltpu.CompilerParams(dimension_semantics=sem, vmem_limit_bytes=VMEM_LIMIT)


def _dot(a, b):
    return jnp.dot(a.astype(BF16), b.astype(BF16), preferred_element_type=F32)


def _dot_nt(a, b):
    return lax.dot_general(a.astype(BF16), b.astype(BF16), (((1,), (1,)), ((), ())),
                           preferred_element_type=F32)


def _dot_tn(a, b):
    return lax.dot_general(a.astype(BF16), b.astype(BF16), (((0,), (0,)), ((), ())),
                           preferred_element_type=F32)


def _split_bf16(x):
    hi = x.astype(BF16)
    return hi, (x - hi.astype(F32)).astype(BF16)


def _sigmoid(x):
    return 1.0 / (1.0 + jnp.exp(-x))


def _group_ones(width, group):
    r = lax.broadcasted_iota(jnp.int32, (width, width), 0) // group
    c = lax.broadcasted_iota(jnp.int32, (width, width), 1) // group
    return (r == c).astype(BF16)


def _resident(a):
    return pl.BlockSpec(a.shape, lambda *_: (0,) * a.ndim, pipeline_mode=pl.Buffered(1))


def _tile_rows(bsz, seq, rows):
    if seq >= rows:
        assert seq % rows == 0
        return 1, rows
    bb = max(1, min(bsz, rows // seq))
    while bsz % bb:
        bb -= 1
    return bb, seq


def _mod_kernel(c_ref, w_ref, b_ref, o_ref):
    c = c_ref[...]
    o_ref[0] = _dot(c * _sigmoid(c), w_ref[0]) + b_ref[0]


def _ada_mod(c_all, ada_w, ada_b):
    depth, d, cols = ada_w.shape
    rows = c_all.shape[0]
    tn = 2048
    return pl.pallas_call(
        _mod_kernel,
        grid=(depth, cols // tn),
        in_specs=[pl.BlockSpec((rows, d), lambda l, j: (0, 0)),
                  pl.BlockSpec((1, d, tn), lambda l, j: (l, 0, j)),
                  pl.BlockSpec((1, 1, tn), lambda l, j: (l, 0, j))],
        out_specs=pl.BlockSpec((1, rows, tn), lambda l, j: (l, 0, j)),
        out_shape=jax.ShapeDtypeStruct((depth, rows, cols), F32),
        compiler_params=_cparams(("parallel", "parallel")),
        name="ada_mod",
    )(c_all, ada_w, ada_b.reshape(depth, 1, cols))


def _inproj_kernel(x_ref, sh_ref, sc_ref, g_ref, w_ref, qg_ref, kg_ref,
                   u_ref, q_ref, k_ref, v_ref, p_ref, gt_ref):
    bb, tt, d = x_ref.shape
    m = bb * tt
    x = x_ref[...]
    ms = jnp.mean(x * x, axis=-1, keepdims=True)
    h = x * lax.rsqrt(ms + NORM_EPS) * g_ref[...]
    h = h * (1.0 + sc_ref[...]) + sh_ref[...]
    hb = h.astype(BF16).reshape(m, d)

    u_ref[...] = jnp.dot(hb, w_ref[:, 0:OFF_Q], preferred_element_type=F32).reshape(bb, tt, POOL_WIDTH)

    zq = jnp.dot(hb, w_ref[:, OFF_Q:OFF_K], preferred_element_type=F32)
    msq = _dot(zq * zq, _group_ones(ATTN_WIDTH, HEAD_DIM)) * (1.0 / HEAD_DIM)
    qn = zq * lax.rsqrt(msq + NORM_EPS) * qg_ref[...] * ATTN_SCALE
    q_ref[...] = qn.astype(BF16).reshape(bb, tt, ATTN_WIDTH)

    zk = jnp.dot(hb, w_ref[:, OFF_K:OFF_V], preferred_element_type=F32)
    msk = _dot(zk * zk, _group_ones(KV_WIDTH, HEAD_DIM)) * (1.0 / HEAD_DIM)
    k_ref[...] = (zk * lax.rsqrt(msk + NORM_EPS) * kg_ref[...]).reshape(bb, tt, KV_WIDTH)

    v_ref[...] = jnp.dot(hb, w_ref[:, OFF_V:OFF_RW], preferred_element_type=F32).reshape(bb, tt, KV_WIDTH)

    p_ref[...] = jnp.dot(hb, w_ref[:, OFF_RW:PAD_GATE], preferred_element_type=F32).reshape(bb, tt, RW_PAD)

    zg = jnp.dot(hb, w_ref[:, PAD_GATE:IN_COLS_PAD], preferred_element_type=F32)
    gt_ref[...] = _sigmoid(zg).astype(BF16).reshape(bb, tt, GATE_COLS)


def _inproj(x, sh, sc, g1, w_in_p, qg, kg):
    bsz, seq, d = x.shape
    bb, tt = _tile_rows(bsz, seq, 512)
    tok = lambda f: pl.BlockSpec((bb, tt, f), lambda b, i: (b, i, 0))
    mod = pl.BlockSpec((bb, 1, d), lambda b, i: (b, 0, 0))
    full = _resident
    return pl.pallas_call(
        _inproj_kernel,
        grid=(bsz // bb, seq // tt),
        in_specs=[tok(d), mod, mod, full(g1), full(w_in_p), full(qg), full(kg)],
        out_specs=[tok(POOL_WIDTH), tok(ATTN_WIDTH), tok(KV_WIDTH), tok(KV_WIDTH), tok(RW_PAD), tok(GATE_COLS)],
        out_shape=[jax.ShapeDtypeStruct((bsz, seq, POOL_WIDTH), F32),
                   jax.ShapeDtypeStruct((bsz, seq, ATTN_WIDTH), BF16),
                   jax.ShapeDtypeStruct((bsz, seq, KV_WIDTH), F32),
                   jax.ShapeDtypeStruct((bsz, seq, KV_WIDTH), F32),
                   jax.ShapeDtypeStruct((bsz, seq, RW_PAD), F32),
                   jax.ShapeDtypeStruct((bsz, seq, GATE_COLS), BF16)],
        compiler_params=_cparams(("parallel", "parallel")),
        name="inproj",
    )(x, sh, sc, g1, w_in_p, qg, kg)


def _pool_kernel(u_ref, uprev_ref, hist_ref, pw_ref, ps_ref, o_ref, *, pos0):
    i = pl.program_id(1)
    tt = u_ref.shape[1]
    halo = uprev_ref.shape[1]
    u = u_ref[0]
    prev = jnp.where(i == 0, hist_ref[0], uprev_ref[0])
    ext = jnp.concatenate([prev, u], axis=0)
    sums = {}
    acc = ext
    for w in POOL_WINDOWS:
        acc = acc + pltpu.roll(acc, w // 2, axis=0)
        sums[w] = acc
    pos = pos0 + i * tt + lax.broadcasted_iota(jnp.int32, (tt, 1), 0)
    outs = []
    for g, w in enumerate(POOL_WINDOWS):
        sl = slice(g * POOL_GROUP, (g + 1) * POOL_GROUP)
        cnt = jnp.minimum(pos + 1, w).astype(F32)
        dlt = sums[w][halo:, sl] / cnt - u[:, sl]
        outs.append(_dot(dlt, pw_ref[g]))
    y = jnp.concatenate(outs, axis=1) * ps_ref[...]
    o_ref[0] = y.astype(BF16)


def _pool(u, hist16, pool_w, pool_scale, pos0):
    bsz, seq, _ = u.shape
    tt = min(seq, 512)
    halo = hist16.shape[1]
    per = tt // halo
    return pl.pallas_call(
        functools.partial(_pool_kernel, pos0=pos0),
        grid=(bsz, seq // tt),
        in_specs=[pl.BlockSpec((1, tt, POOL_WIDTH), lambda b, i: (b, i, 0)),
                  pl.BlockSpec((1, halo, POOL_WIDTH), lambda b, i: (b, jnp.maximum(i * per - 1, 0), 0)),
                  pl.BlockSpec((1, halo, POOL_WIDTH), lambda b, i: (b, 0, 0)),
                  pl.BlockSpec(pool_w.shape, lambda b, i: (0, 0, 0)),
                  pl.BlockSpec(pool_scale.shape, lambda b, i: (0, 0))],
        out_specs=pl.BlockSpec((1, tt, POOL_WIDTH), lambda b, i: (b, i, 0)),
        out_shape=jax.ShapeDtypeStruct((bsz, seq, POOL_WIDTH), BF16),
        compiler_params=_cparams(("parallel", "arbitrary")),
        name="pool",
    )(u, u, hist16, pool_w, pool_scale)


def _attn_core(q, kcat, vcat, bias, sink_ref):
    lo = lax.broadcasted_iota(jnp.int32, (1, KV_WIDTH), 1) < HEAD_DIM
    krot = pltpu.roll(kcat, HEAD_DIM, axis=1)
    vrot = pltpu.roll(vcat, HEAD_DIM, axis=1)
    khs, vhs, qps, sinks = [], [], [], []
    for g in range(N_KV):
        ksrc_lo, ksrc_hi = (kcat, krot) if g == 0 else (krot, kcat)
        vsrc_lo, vsrc_hi = (vcat, vrot) if g == 0 else (vrot, vcat)
        k_lo = jnp.where(lo, ksrc_lo, 0.0).astype(BF16)
        k_hi = jnp.where(lo, 0.0, ksrc_hi).astype(BF16)
        v_lo = jnp.where(lo, vsrc_lo, 0.0).astype(BF16)
        v_hi = jnp.where(lo, 0.0, vsrc_hi).astype(BF16)
        for jj in range(2):
            pair = 2 * g + jj
            qp = q[:, pair * LANES:(pair + 1) * LANES]
            for half, (kh, vh) in enumerate(((k_lo, v_lo), (k_hi, v_hi))):
                khs.append(kh)
                vhs.append(vh)
                qps.append(qp)
                sinks.append(sink_ref[2 * pair + half])
    ss = [_dot_nt(qp, kh) + bias for qp, kh in zip(qps, khs)]
    mxs = [jnp.maximum(jnp.max(s, axis=-1, keepdims=True), sk) for s, sk in zip(ss, sinks)]
    prs = [jnp.exp(s - mx) for s, mx in zip(ss, mxs)]
    inv = [1.0 / (jnp.sum(pr, axis=-1, keepdims=True) + jnp.exp(sk - mx))
           for pr, sk, mx in zip(prs, sinks, mxs)]
    ohs = [jnp.dot(pr.astype(BF16), vh, preferred_element_type=F32) * iv
           for pr, vh, iv in zip(prs, vhs, inv)]
    return jnp.concatenate([ohs[2 * p] + ohs[2 * p + 1] for p in range(N_HEADS // 2)], axis=1)


def _attn_banded_kernel(sink_ref, q_ref, kp_ref, kc_ref, vp_ref, vc_ref, band_ref, o_ref):
    i = pl.program_id(1)
    kcat = jnp.concatenate([kp_ref[0], kc_ref[0]], axis=0)
    vcat = jnp.concatenate([vp_ref[0], vc_ref[0]], axis=0)
    col = lax.broadcasted_iota(jnp.int32, (1, kcat.shape[0]), 1)
    first = jnp.where((col < WINDOW) & (i == 0), NEG_BIG, 0.0)
    bias = band_ref[...] + first
    q = q_ref[0]
    tq = q.shape[0]
    parts = []
    for r0 in range(0, tq, WINDOW):
        rows, cols = slice(r0, r0 + WINDOW), slice(r0, r0 + 2 * WINDOW)
        parts.append(_attn_core(q[rows], kcat[cols], vcat[cols], bias[rows, cols], sink_ref))
    o_ref[0] = jnp.concatenate(parts, axis=0).astype(BF16)


def _band_bias(tq):
    qc = jnp.arange(tq)[:, None] // CHUNK
    kc = jnp.arange(WINDOW + tq)[None, :] // CHUNK
    win = WINDOW // CHUNK
    return jnp.where((kc >= qc) & (kc <= qc + win), 0.0, NEG_BIG).astype(F32)


def _attn_banded(q, k, v, sink):
    bsz, seq, _ = q.shape
    tq = min(seq, ATTN_TILE_CHUNKS * CHUNK)
    assert seq % tq == 0 and tq % WINDOW == 0
    per = tq // WINDOW
    band = _band_bias(tq)
    prev = pl.BlockSpec((1, WINDOW, KV_WIDTH), lambda b, i: (b, jnp.maximum(i * per - 1, 0), 0))
    cur = pl.BlockSpec((1, tq, KV_WIDTH), lambda b, i: (b, i, 0))
    return pl.pallas_call(
        _attn_banded_kernel,
        grid=(bsz, seq // tq),
        in_specs=[pl.BlockSpec(memory_space=pltpu.SMEM),
                  pl.BlockSpec((1, tq, ATTN_WIDTH), lambda b, i: (b, i, 0)),
                  prev, cur, prev, cur,
                  pl.BlockSpec(band.shape, lambda b, i: (0, 0))],
        out_specs=pl.BlockSpec((1, tq, ATTN_WIDTH), lambda b, i: (b, i, 0)),
        out_shape=jax.ShapeDtypeStruct((bsz, seq, ATTN_WIDTH), BF16),
        compiler_params=_cparams(("parallel", "parallel")),
        name="attn_banded",
    )(sink, q, k, k, v, v, band)


def _attn_cached_kernel(sink_ref, q_ref, ck_ref, k_ref, cv_ref, v_ref, o_ref):
    cache, new = ck_ref.shape[1], k_ref.shape[1]
    span = 3 * CHUNK
    pad = jnp.zeros((span - cache - new, KV_WIDTH), F32)
    kcat = jnp.concatenate([ck_ref[0], k_ref[0], pad], axis=0)
    vcat = jnp.concatenate([cv_ref[0], v_ref[0], pad], axis=0)
    col = lax.broadcasted_iota(jnp.int32, (1, span), 1)
    bias = jnp.where(col < cache + new, 0.0, NEG_BIG)
    o_ref[0] = _attn_core(q_ref[0], kcat, vcat, bias, sink_ref).astype(BF16)


def _attn_cached(q, k, v, ck, cv, sink):
    bsz, seq, _ = q.shape
    cache = ck.shape[1]
    assert cache + seq <= 3 * CHUNK and (3 * CHUNK - cache - seq) % 8 == 0
    blk = lambda t, f: pl.BlockSpec((1, t, f), lambda b: (b, 0, 0))
    return pl.pallas_call(
        _attn_cached_kernel,
        grid=(bsz,),
        in_specs=[pl.BlockSpec(memory_space=pltpu.SMEM), blk(seq, ATTN_WIDTH),
                  blk(cache, KV_WIDTH), blk(seq, KV_WIDTH), blk(cache, KV_WIDTH), blk(seq, KV_WIDTH)],
        out_specs=blk(seq, ATTN_WIDTH),
        out_shape=jax.ShapeDtypeStruct((bsz, seq, ATTN_WIDTH), BF16),
        compiler_params=_cparams(("parallel",)),
        name="attn_cached",
    )(sink, q, ck, k, cv, v)


def _rwkv_kernel(p_ref, pprev_ref, hist_ref, s0_ref, mu_ref, vec_ref, wwa_ref, g2_ref,
                 ones_ref, mbd_ref, mbd2_ref, tri_ref, mst_ref,
                 o_ref, sout_ref, s_scr, *, chunk):
    i = pl.program_id(1)
    tt = p_ref.shape[1]
    c_len = chunk
    n_chunks = tt // c_len
    hp = HEADS_PER_PACK
    wide = hp * c_len
    n_pack = RW_WIDTH // PACK

    @pl.when(i == 0)
    def _():
        s_scr[...] = s0_ref[0]

    p = p_ref[0]
    prev_row = jnp.where(i == 0, hist_ref[0], pprev_ref[0, 7:8, :])
    row = lax.broadcasted_iota(jnp.int32, (tt, 1), 0)
    prev = jnp.where(row == 0, prev_row, pltpu.roll(p, 1, axis=0))
    pm = p + (prev - p) * mu_ref[...]

    w0, a0, k_k, k_a = vec_ref[0:1], vec_ref[1:2], vec_ref[2:3], vec_ref[3:4]
    r_k, ln_w, ln_b = vec_ref[4:5], vec_ref[5:6], vec_ref[6:7]
    r = pm[:, 0:RW_WIDTH]
    k = pm[:, RW_WIDTH:2 * RW_WIDTH]
    v = pm[:, 2 * RW_WIDTH:3 * RW_WIDTH]
    xa = pm[:, RW_LORA_OFF:RW_LORA_OFF + LANES]
    lane = lax.broadcasted_iota(jnp.int32, (1, LANES), 1)
    xa = jnp.where(lane < RW_DECAY_LORA, jnp.tanh(xa), xa)
    la = _dot(xa, wwa_ref[...])
    wlin = -(w0 + la[:, :RW_WIDTH])
    softplus = jnp.maximum(wlin, 0.0) + jnp.log(1.0 + jnp.exp(-jnp.abs(wlin)))
    lw = -jnp.exp(-softplus - 0.5)
    a = _sigmoid(a0 + la[:, RW_WIDTH:])
    g = _dot(_sigmoid(pm[:, RW_GATE_OFF:RW_PAD]), g2_ref[...])
    ones_g = ones_ref[...]
    kk = k * k_k
    kk = kk / jnp.maximum(jnp.sqrt(_dot(kk * kk, ones_g)), 1e-12)
    k2 = k * (1.0 + (a - 1.0) * k_a)
    bonus = _dot(r * k2 * r_k, ones_g) * v
    kka = kk * a


    def decay_gen(c0, c1, out):
        rows = slice(c0 * c_len, c1 * c_len)
        lw_w = lw[rows]
        in_chunk = lax.broadcasted_iota(jnp.int32, ((c1 - c0) * c_len, 1), 0) % c_len
        cs = lw_w
        sft = 1
        while sft < c_len:
            cs = cs + jnp.where(in_chunk >= sft, pltpu.roll(cs, sft, axis=0), 0.0)
            sft *= 2
            yield
        tot = jnp.concatenate(
            [jnp.broadcast_to(cs[(c + 1) * c_len - 1:(c + 1) * c_len, :], (c_len, RW_WIDTH))
             for c in range(c1 - c0)], axis=0)
        p_inv = jnp.exp(-cs)
        out.update(tot=tot, v=v[rows], bt=kka[rows] * p_inv, kt=k2[rows] * p_inv)
        yield
        out.update(at=-kk[rows] * jnp.exp(cs - lw_w), rt=r[rows] * jnp.exp(cs))
        yield
        p_end = jnp.exp(tot - cs)
        out.update(bb=kka[rows] * p_end, kb=k2[rows] * p_end)
        yield

    mbd, mbd2 = mbd_ref[...], mbd2_ref[...]
    strict, lower, eye_p = tri_ref[0], tri_ref[1], tri_ref[2]
    mask_state, eye_state = mst_ref[0], mst_ref[1]
    bd = lambda x: jnp.concatenate([x.astype(BF16)] * hp, axis=0) * mbd
    bd2 = lambda x: jnp.concatenate([x.astype(BF16)] * hp, axis=0) * mbd2
    zeros_cp = jnp.zeros((c_len, PACK), F32)

    view = lambda x, u: x[u[0] * c_len:(u[0] + 1) * c_len, u[1] * PACK:(u[1] + 1) * PACK]

    def prepass_gen(n_wave_chunks, arr):
        units = [(c, j) for c in range(n_wave_chunks) for j in range(n_pack)]
        at, rt, bt, kt, vv = arr['at'], arr['rt'], arr['bt'], arr['kt'], arr['v']
        s1 = [_dot_nt(jnp.concatenate([view(at, u), view(rt, u)], axis=0),
                      jnp.concatenate([bd(view(bt, u)), bd(view(kt, u))], axis=0)) for u in units]
        yield
        l_ab = [s[:c_len, :wide] * strict for s in s1]
        l_ak = [s[:c_len, wide:] * strict for s in s1]
        g_bk = [jnp.concatenate([s[c_len:, :wide] * lower, s[c_len:, wide:] * lower], axis=1) for s in s1]
        pw = [_dot(l, bd2(l)) for l in l_ab]
        tm = [eye_p + l for l in l_ab]
        yield
        span = 2
        while span < c_len:
            rhs = [bd2(x) for x in pw]
            if 2 * span < c_len:
                both = [_dot(jnp.concatenate([x, t], axis=0), rr) for x, t, rr in zip(pw, tm, rhs)]
                pw = [bo[:c_len] for bo in both]
                tm = [t + bo[c_len:] for t, bo in zip(tm, both)]
            else:
                tm = [t + _dot(t, rr) for t, rr in zip(tm, rhs)]
            span *= 2
            yield
        bdv = [bd(view(vv, u)) for u in units]
        x1 = [_dot(l, bv) for l, bv in zip(l_ak, bdv)]
        yield
        wu = [_dot(t, jnp.concatenate([bd(view(at, u)), bd(x)], axis=1)) for t, u, x in zip(tm, units, x1)]
        yield
        gw = [_dot(gg[:, :wide], jnp.concatenate([bd(w[:, :PACK]), bd(w[:, PACK:])], axis=1))
              for gg, w in zip(g_bk, wu)]
        gv = [_dot(gg[:, wide:], bv) for gg, bv in zip(g_bk, bdv)]
        arr['gy'] = [jnp.concatenate([a_[:, :PACK], a_[:, PACK:] + b_], axis=1) for a_, b_ in zip(gw, gv)]
        yield
        arr['mn'] = [_dot_tn(jnp.concatenate([view(arr['bb'], u), view(arr['kb'], u)], axis=0),
                             jnp.concatenate([w, jnp.concatenate([zeros_cp, view(vv, u)], axis=1)], axis=0))
                     for u, w in zip(units, wu)]
        yield

    states = [s_scr[j] for j in range(n_pack)]
    y_rows = []

    def state_gen(n_wave_chunks, arr):
        for c in range(n_wave_chunks):
            y_halves = []
            for j in range(n_pack):
                u = (c, j)
                gy, mn = arr['gy'][c * n_pack + j], arr['mn'][c * n_pack + j]
                p_tot = jnp.exp(arr['tot'][c * c_len:c * c_len + 1, j * PACK:(j + 1) * PACK])
                rh = view(arr['rt'], u) + gy[:, :PACK]
                mc = mn[:, :PACK] * mask_state + eye_state * p_tot
                mc_hi, mc_lo = _split_bf16(mc)
                st_hi, st_lo = _split_bf16(states[j])
                main = jnp.dot(jnp.concatenate([rh.astype(BF16), mc_hi, mc_lo], axis=0), st_hi,
                               preferred_element_type=F32)
                y_halves.append(main[:c_len] + gy[:, PACK:])
                states[j] = (main[c_len:c_len + PACK] + main[c_len + PACK:]
                             + jnp.dot(mc_hi, st_lo, preferred_element_type=F32)
                             + mn[:, PACK:] * mask_state)
            y_rows.append(jnp.concatenate(y_halves, axis=1))
            yield

    def drain(*gens):
        gens = list(gens)
        while gens:
            for gen in list(gens):
                if next(gen, gens) is gens:
                    gens.remove(gen)

    n_waves = RWKV_WAVES if n_chunks % RWKV_WAVES == 0 else 1
    wc = n_chunks // n_waves
    arrs = [dict() for _ in range(n_waves)]
    drain(decay_gen(0, wc, arrs[0]))
    for w in range(n_waves):
        side = []
        if w + 1 < n_waves:
            side.append(decay_gen((w + 1) * wc, (w + 2) * wc, arrs[w + 1]))
        if w > 0:
            side.append(state_gen(wc, arrs[w - 1]))
        drain(prepass_gen(wc, arrs[w]), *side)
    drain(state_gen(wc, arrs[n_waves - 1]))
    for j in range(n_pack):
        s_scr[j] = states[j]
    y = y_rows[0] if n_chunks == 1 else jnp.concatenate(y_rows, axis=0)

    inv_hd = 1.0 / RW_HD
    dy = y - _dot(y, ones_g) * inv_hd
    var = _dot(dy * dy, ones_g) * inv_hd
    yn = dy * lax.rsqrt(var + RW_GN_EPS) * ln_w + ln_b
    o_ref[0] = ((yn + bonus) * g).astype(BF16)

    @pl.when(i == pl.num_programs(1) - 1)
    def _():
        sout_ref[0] = s_scr[...]


def _rwkv_masks(c_len):
    wide = HEADS_PER_PACK * c_len
    r_blk = jnp.arange(wide)[:, None] // c_len
    mbd = (r_blk == jnp.arange(PACK)[None, :] // RW_HD).astype(BF16)
    mbd2 = (r_blk == jnp.arange(wide)[None, :] // c_len).astype(BF16)
    t_idx = jnp.arange(c_len)[:, None]
    s_idx = jnp.arange(wide)[None, :] % c_len
    tri = jnp.stack([s_idx < t_idx, s_idx <= t_idx, s_idx == t_idx]).astype(F32)
    st = jnp.arange(PACK)
    mst = jnp.stack([(st[:, None] // RW_HD) == (st[None, :] // RW_HD), st[:, None] == st[None, :]]).astype(F32)
    g_idx = jnp.arange(RW_WIDTH) // RW_HD
    ones_g = (g_idx[:, None] == g_idx[None, :]).astype(BF16)
    return ones_g, mbd, mbd2, tri, mst


def _rwkv(p, hist, s0_packed, mu, vec, wwa, g2p, chunks_per_tile):
    bsz, seq, _ = p.shape
    chunk = min(CHUNK, seq)
    tt = min(seq, chunk * chunks_per_tile)
    per = tt // 8
    npk = RW_WIDTH // PACK
    consts = _rwkv_masks(chunk)
    full = _resident
    return pl.pallas_call(
        functools.partial(_rwkv_kernel, chunk=chunk),
        grid=(bsz, seq // tt),
        in_specs=[pl.BlockSpec((1, tt, RW_PAD), lambda b, i: (b, i, 0)),
                  pl.BlockSpec((1, 8, RW_PAD), lambda b, i: (b, jnp.maximum(i * per - 1, 0), 0)),
                  pl.BlockSpec((1, 1, RW_PAD), lambda b, i: (b, 0, 0)),
                  pl.BlockSpec((1, npk, PACK, PACK), lambda b, i: (b, 0, 0, 0)),
                  full(mu), full(vec), full(wwa), full(g2p)] + [full(c) for c in consts],
        out_specs=[pl.BlockSpec((1, tt, RW_WIDTH), lambda b, i: (b, i, 0)),
                   pl.BlockSpec((1, npk, PACK, PACK), lambda b, i: (b, 0, 0, 0))],
        out_shape=[jax.ShapeDtypeStruct((bsz, seq, RW_WIDTH), BF16),
                   jax.ShapeDtypeStruct((bsz, npk, PACK, PACK), F32)],
        scratch_shapes=[pltpu.VMEM((npk, PACK, PACK), F32)],
        compiler_params=_cparams(("parallel", "arbitrary")),
        name="rwkv",
    )(p, p, hist, s0_packed, mu, vec, wwa, g2p, *consts)


def _pack_state(wkv):
    bsz = wkv.shape[0]
    npk = RW_HEADS // HEADS_PER_PACK
    st = jnp.swapaxes(wkv, -1, -2).reshape(bsz, npk, HEADS_PER_PACK, RW_HD, RW_HD)
    eye = jnp.eye(HEADS_PER_PACK, dtype=wkv.dtype)
    return jnp.einsum('bjhkv,hg->bjhkgv', st, eye).reshape(bsz, npk, PACK, PACK)


def _unpack_state(sp):
    bsz, npk = sp.shape[:2]
    s6 = sp.reshape(bsz, npk, HEADS_PER_PACK, RW_HD, HEADS_PER_PACK, RW_HD)
    blocks = jnp.stack([s6[:, :, h, :, h, :] for h in range(HEADS_PER_PACK)], axis=2)
    return jnp.swapaxes(blocks.reshape(bsz, RW_HEADS, RW_HD, RW_HD), -1, -2)


def _merge_kernel(x_ref, op_ref, oa_ref, or_ref, gt_ref, ga_ref, sh_ref, sc_ref, g2_ref,
                  pp_ref, pa_ref, pr_ref, wo_ref, x1_ref, h2_ref):
    bb, tt, d = x_ref.shape
    m = bb * tt
    gates = gt_ref[...].reshape(m, GATE_COLS)
    merged = None
    for n, (o_ref, w_ref) in enumerate(((op_ref, pp_ref), (oa_ref, pa_ref), (or_ref, pr_ref))):
        o = o_ref[...].reshape(m, o_ref.shape[2])
        term = gates[:, n * d:(n + 1) * d].astype(F32) * jnp.dot(o, w_ref[...], preferred_element_type=F32)
        merged = term if merged is None else merged + term
    y = _dot(merged, wo_ref[...]).reshape(bb, tt, d)
    x1 = x_ref[...] + ga_ref[...] * y
    x1_ref[...] = x1
    ms = jnp.mean(x1 * x1, axis=-1, keepdims=True)
    h = x1 * lax.rsqrt(ms + NORM_EPS) * g2_ref[...]
    h2_ref[...] = (h * (1.0 + sc_ref[...]) + sh_ref[...]).astype(BF16)


def _merge(x, o_pool, o_attn, o_rw, gates, ga1, sh2, sc2, g2, pp, pa, pr, wo):
    bsz, seq, d = x.shape
    bb, tt = _tile_rows(bsz, seq, 512)
    tok = lambda f: pl.BlockSpec((bb, tt, f), lambda b, i: (b, i, 0))
    mod = pl.BlockSpec((bb, 1, d), lambda b, i: (b, 0, 0))
    full = _resident
    return pl.pallas_call(
        _merge_kernel,
        grid=(bsz // bb, seq // tt),
        in_specs=[tok(d), tok(POOL_WIDTH), tok(ATTN_WIDTH), tok(RW_WIDTH), tok(GATE_COLS),
                  mod, mod, mod, full(g2), full(pp), full(pa), full(pr), full(wo)],
        out_specs=[tok(d), tok(d)],
        out_shape=[jax.ShapeDtypeStruct((bsz, seq, d), F32), jax.ShapeDtypeStruct((bsz, seq, d), BF16)],
        compiler_params=_cparams(("parallel", "parallel")),
        name="merge",
    )(x, o_pool, o_attn, o_rw, gates, ga1, sh2, sc2, g2, pp, pa, pr, wo)


def _swiglu_rows(h, w1, w3, w2):
    a = jnp.dot(h, w1, preferred_element_type=F32)
    b = jnp.dot(h, w3, preferred_element_type=F32)
    return jnp.dot((a * _sigmoid(a) * b).astype(BF16), w2, preferred_element_type=F32)


def _ffn_kernel(x_ref, h_ref, ga_ref, w1_ref, w3_ref, w2_ref, o_ref, *, n_split):
    bb, tt, d = x_ref.shape
    h = h_ref[...].reshape(bb * tt, d)
    ff = w1_ref.shape[1]
    step = ff // n_split
    f = None
    for s in range(n_split):
        sl = slice(s * step, (s + 1) * step)
        part = _swiglu_rows(h, w1_ref[:, sl], w3_ref[:, sl], w2_ref[sl, :])
        f = part if f is None else f + part
    o_ref[...] = x_ref[...] + ga_ref[...] * f.reshape(bb, tt, d)


def _ffn(x, h, ga, w1, w3, w2):
    bsz, seq, d = x.shape
    bb, tt = _tile_rows(bsz, seq, 512)
    tok = pl.BlockSpec((bb, tt, d), lambda b, i: (b, i, 0))
    mod = pl.BlockSpec((bb, 1, d), lambda b, i: (b, 0, 0))
    full = _resident
    return pl.pallas_call(
        functools.partial(_ffn_kernel, n_split=2),
        grid=(bsz // bb, seq // tt),
        in_specs=[tok, tok, mod, full(w1), full(w3), full(w2)],
        out_specs=tok,
        out_shape=jax.ShapeDtypeStruct((bsz, seq, d), F32),
        compiler_params=_cparams(("parallel", "parallel")),
        name="ffn",
    )(x, h, ga, w1, w3, w2)


def _router_kernel(h_ref, rw_ref, rb_ref, tri_ref, gate_ref, rank_ref, cnt_ref):
    bb, tt, d = h_ref.shape
    m = bb * tt
    h = h_ref[...].reshape(m, d)
    lane = lax.broadcasted_iota(jnp.int32, (m, LANES), 1)
    logits = _dot(h, rw_ref[...]) + rb_ref[...]
    logits = jnp.where(lane < N_EXPERTS, logits, NEG_BIG)
    mx = jnp.max(logits, axis=-1, keepdims=True)
    ex = jnp.exp(logits - mx)
    probs = ex / jnp.sum(ex, axis=-1, keepdims=True)
    p1 = jnp.max(probs, axis=-1, keepdims=True)
    i1 = jnp.min(jnp.where(probs == p1, lane, LANES), axis=-1, keepdims=True)
    rest = jnp.where(lane == i1, -1.0, probs)
    p2 = jnp.max(rest, axis=-1, keepdims=True)
    i2 = jnp.min(jnp.where(rest == p2, lane, LANES), axis=-1, keepdims=True)
    tot = p1 + p2
    gate = jnp.where(lane == i1, p1 / tot, 0.0) + jnp.where(lane == i2, p2 / tot, 0.0)
    gate_ref[...] = gate.reshape(bb, tt, LANES)
    assigned = jnp.where((lane == i1) | (lane == i2), 1.0, 0.0)
    rr = _dot_tn(assigned, tri_ref[...])
    rank = jnp.where(rr[:, m:] > 0.5, rr[:, :m], -1.0)
    rank_ref[0] = rank[:N_EXPERTS].astype(jnp.int32)
    cnt = jnp.sum(assigned, axis=0, keepdims=True)
    cnt_ref[0] = jnp.broadcast_to(cnt, (8, LANES)).astype(jnp.int32)


def _moe_kernel(cnt_ref, x_ref, h_ref, ga_ref, gate_ref, rank_ref, w1_ref, w3_ref, w2_ref, o_ref):
    e = pl.program_id(2)
    tile = pl.program_id(0) * pl.num_programs(1) + pl.program_id(1)
    bb, tt, d = x_ref.shape
    m = bb * tt

    @pl.when(e == 0)
    def _():
        o_ref[...] = x_ref[...]

    count = cnt_ref[tile * N_EXPERTS + e]
    h = h_ref[...].reshape(m, d)
    lane = lax.broadcasted_iota(jnp.int32, (m, LANES), 1)
    gate = jnp.sum(jnp.where(lane == e, gate_ref[...].reshape(m, LANES), 0.0), axis=-1, keepdims=True)
    rank_row = rank_ref[0, pl.ds(e, 1), :]

    def expert_rows(start, rows):
        row = lax.broadcasted_iota(jnp.int32, (rows, m), 0)
        sel = jnp.where(rank_row == row + start, 1.0, 0.0).astype(BF16)
        xs = jnp.dot(sel, h, preferred_element_type=F32).astype(BF16)
        f = _swiglu_rows(xs, w1_ref[0], w3_ref[0], w2_ref[0])
        o_ref[...] += ga_ref[...] * (gate * _dot_tn(sel, f)).reshape(bb, tt, d)

    half = MOE_ROWS // 2
    rem = count % MOE_ROWS
    n_full = count // MOE_ROWS + jnp.where(rem > half, 1, 0)

    def full_block(kb, carry):
        expert_rows(kb * MOE_ROWS, MOE_ROWS)
        return carry

    lax.fori_loop(0, n_full, full_block, 0)

    @pl.when((rem > 0) & (rem <= half))
    def _():
        expert_rows(n_full * MOE_ROWS, half)


def _moe(x, h, ga, rw_p, rb_p, w1, w3, w2):
    bsz, seq, d = x.shape
    bb, tt = _tile_rows(bsz, seq, MOE_TILE)
    m = bb * tt
    nb, nt = bsz // bb, seq // tt
    n_e, _, ffe = w1.shape
    t_i = jnp.arange(m)
    tri = jnp.concatenate([t_i[:, None] < t_i[None, :], t_i[:, None] == t_i[None, :]], axis=1).astype(BF16)
    gate, rank, cnt = pl.pallas_call(
        _router_kernel,
        grid=(nb, nt),
        in_specs=[pl.BlockSpec((bb, tt, d), lambda b, i: (b, i, 0)),
                  pl.BlockSpec(rw_p.shape, lambda b, i: (0, 0)),
                  pl.BlockSpec(rb_p.shape, lambda b, i: (0, 0)),
                  pl.BlockSpec(tri.shape, lambda b, i: (0, 0))],
        out_specs=[pl.BlockSpec((bb, tt, LANES), lambda b, i: (b, i, 0)),
                   pl.BlockSpec((1, N_EXPERTS, m), lambda b, i: (b * nt + i, 0, 0)),
                   pl.BlockSpec((1, 8, LANES), lambda b, i: (b * nt + i, 0, 0))],
        out_shape=[jax.ShapeDtypeStruct((bsz, seq, LANES), F32),
                   jax.ShapeDtypeStruct((nb * nt, N_EXPERTS, m), jnp.int32),
                   jax.ShapeDtypeStruct((nb * nt, 8, LANES), jnp.int32)],
        compiler_params=_cparams(("parallel", "parallel")),
        name="router",
    )(h, rw_p, rb_p, tri)
    counts = cnt[:, 0, :N_EXPERTS].reshape(-1)
    tok = lambda f: pl.BlockSpec((bb, tt, f), lambda b, i, e, c: (b, i, 0))
    return pl.pallas_call(
        _moe_kernel,
        grid_spec=pltpu.PrefetchScalarGridSpec(
            num_scalar_prefetch=1,
            grid=(nb, nt, n_e),
            in_specs=[tok(d), tok(d), pl.BlockSpec((bb, 1, d), lambda b, i, e, c: (b, 0, 0)), tok(LANES),
                      pl.BlockSpec((1, N_EXPERTS, m), lambda b, i, e, c: (b * nt + i, 0, 0)),
                      pl.BlockSpec((1, d, ffe), lambda b, i, e, c: (e, 0, 0)),
                      pl.BlockSpec((1, d, ffe), lambda b, i, e, c: (e, 0, 0)),
                      pl.BlockSpec((1, ffe, d), lambda b, i, e, c: (e, 0, 0))],
            out_specs=tok(d)),
        out_shape=jax.ShapeDtypeStruct((bsz, seq, d), F32),
        compiler_params=_cparams(("parallel", "parallel", "arbitrary")),
        name="moe",
    )(counts, x, h, ga, gate, rank, w1, w3, w2)


def _prep_weights(W):
    depth = W['w_in'].shape[0]
    P = {}
    w_in = W['w_in']
    pad = jnp.zeros((depth, D_MODEL, RW_PAD - RW_IN), w_in.dtype)
    P['w_in'] = jnp.concatenate([w_in[:, :, :OFF_GATE], pad, w_in[:, :, OFF_GATE:]], axis=-1).astype(BF16)
    P['qg'] = jnp.tile(W['q_norm_g'], (1, N_HEADS)).reshape(depth, 1, ATTN_WIDTH)
    P['kg'] = jnp.tile(W['k_norm_g'], (1, N_KV)).reshape(depth, 1, KV_WIDTH)
    P['g1'] = W['norm1_g'].reshape(depth, 1, D_MODEL)
    P['g2'] = W['norm2_g'].reshape(depth, 1, D_MODEL)
    P['pool_w'] = W['pool_w'].astype(BF16)
    P['pool_scale'] = W['pool_scale'].reshape(depth, 1, POOL_WIDTH)
    P['mu'] = jnp.pad(W['rw_mu'], ((0, 0), (0, RW_PAD - RW_IN))).reshape(depth, 1, RW_PAD)
    vec = jnp.stack([W['rw_w0'], W['rw_a0'], W['rw_k_k'], W['rw_k_a'],
                     W['rw_r_k'].reshape(depth, RW_WIDTH), W['rw_ln_w'], W['rw_ln_b'],
                     jnp.zeros_like(W['rw_w0'])], axis=1)
    P['vec'] = vec
    zl = jnp.zeros_like(W['rw_w2'])
    P['wwa'] = jnp.concatenate([jnp.concatenate([W['rw_w2'], zl], axis=2),
                                jnp.concatenate([zl, W['rw_a2']], axis=2)], axis=1).astype(BF16)
    P['g2p'] = jnp.pad(W['rw_g2'], ((0, 0), (0, RW_PAD - RW_GATE_OFF - RW_GATE_LORA), (0, 0))).astype(BF16)
    for name in ('proj_pool', 'proj_attn', 'proj_rwkv', 'w_out', 'ffn_w1', 'ffn_w3', 'ffn_w2',
                 'moe_w1', 'moe_w3', 'moe_w2'):
        P[name] = W[name].astype(BF16)
    P['router_w'] = jnp.pad(W['router_w'], ((0, 0), (0, 0), (0, LANES - N_EXPERTS))).astype(BF16)
    P['router_b'] = jnp.pad(W['router_b'], ((0, 0), (0, LANES - N_EXPERTS))).reshape(-1, 1, LANES)
    P['sink'] = W['attn_sink']
    return P


def _layer(x, mods, l, P, pos0, hist_pool, hist_shift, wkv0, ck, cv, chunks_per_tile):
    bsz, seq, _ = x.shape
    sh1, sc1, ga1, sh2, sc2, ga2 = mods
    u, q, k, v, p, gates = _inproj(x, sh1, sc1, P['g1'][l], P['w_in'][l], P['qg'][l], P['kg'][l])
    hist16 = jnp.pad(hist_pool, ((0, 0), (1, 0), (0, 0)))
    o_pool = _pool(u, hist16, P['pool_w'][l], P['pool_scale'][l], pos0)
    if ck is None:
        o_attn = _attn_banded(q, k, v, P['sink'][l])
        new_k, new_v = k[:, -WINDOW:], v[:, -WINDOW:]
    else:
        o_attn = _attn_cached(q, k, v, ck.reshape(bsz, -1, KV_WIDTH), cv.reshape(bsz, -1, KV_WIDTH), P['sink'][l])
        new_k, new_v = k, v
    hist_p = jnp.pad(hist_shift, ((0, 0), (0, 0), (0, RW_PAD - RW_IN)))
    o_rw, s_out = _rwkv(p, hist_p, wkv0, P['mu'][l], P['vec'][l], P['wwa'][l], P['g2p'][l],
                        chunks_per_tile)
    x1, h2 = _merge(x, o_pool, o_attn, o_rw, gates, ga1, sh2, sc2, P['g2'][l],
                    P['proj_pool'][l], P['proj_attn'][l], P['proj_rwkv'][l], P['w_out'][l])
    i = l // 2
    if l % 2 == 0:
        x2 = _ffn(x1, h2, ga2, P['ffn_w1'][i], P['ffn_w3'][i], P['ffn_w2'][i])
    else:
        x2 = _moe(x1, h2, ga2, P['router_w'][i], P['router_b'][i], P['moe_w1'][i], P['moe_w3'][i], P['moe_w2'][i])
    new_pool = jnp.concatenate([hist_pool, u], axis=1)[:, -POOL_HIST:]
    return x2, (new_k, new_v, new_pool, p[:, -1:, :RW_IN], s_out)


def _trunk(x, mod_all, P, pos0, states, chunks_per_tile):
    bsz = x.shape[0]
    depth = P['w_in'].shape[0]
    outs = ([], [], [], [], [])
    n_pack = RW_HEADS // HEADS_PER_PACK
    if states is not None:
        wkv_in = states[4]
        packed_in = _pack_state(wkv_in.reshape(depth * bsz, *wkv_in.shape[2:])).reshape(depth, bsz, n_pack, PACK, PACK)
    for l in range(depth):
        mods = tuple(mod_all[l, :, n * D_MODEL:(n + 1) * D_MODEL].reshape(bsz, 1, D_MODEL) for n in range(6))
        if states is None:
            hp = jnp.zeros((bsz, POOL_HIST, POOL_WIDTH), F32)
            hs = jnp.zeros((bsz, 1, RW_IN), F32)
            s0 = jnp.zeros((bsz, n_pack, PACK, PACK), F32)
            ck = cv = None
        else:
            ck, cv, hp, hs = (s[l] for s in states[:4])
            s0 = packed_in[l]
        x, new = _layer(x, mods, l, P, pos0, hp, hs, s0, ck, cv, chunks_per_tile)
        for lst, arr in zip(outs, new):
            lst.append(arr)
    ks, vs, pools, shifts, packed = (jnp.stack(lst) for lst in outs)
    heads = lambda t: t.reshape(*t.shape[:-1], N_KV, HEAD_DIM)
    wkv = _unpack_state(packed.reshape(depth * bsz, *packed.shape[2:]))
    return x, [heads(ks), heads(vs), pools, shifts, wkv.reshape(depth, bsz, *wkv.shape[1:])]


def kernel(x_prompt, x_sample, c_prompt, c_sample, cache_attn_k, cache_attn_v, state_pool, state_rwkv_shift, state_rwkv_wkv, norm1_g, norm2_g, ada_w, ada_b, w_in, pool_w, pool_scale, q_norm_g, k_norm_g, attn_sink, rw_mu, rw_w0, rw_w2, rw_a0, rw_a2, rw_g2, rw_k_k, rw_k_a, rw_r_k, rw_ln_w, rw_ln_b, proj_pool, proj_attn, proj_rwkv, w_out, ffn_w1, ffn_w3, ffn_w2, router_w, router_b, moe_w1, moe_w3, moe_w2):
    W = dict(norm1_g=norm1_g, norm2_g=norm2_g, w_in=w_in, pool_w=pool_w,
             pool_scale=pool_scale, q_norm_g=q_norm_g, k_norm_g=k_norm_g, attn_sink=attn_sink,
             rw_mu=rw_mu, rw_w0=rw_w0, rw_w2=rw_w2, rw_a0=rw_a0, rw_a2=rw_a2, rw_g2=rw_g2,
             rw_k_k=rw_k_k, rw_k_a=rw_k_a, rw_r_k=rw_r_k, rw_ln_w=rw_ln_w, rw_ln_b=rw_ln_b,
             proj_pool=proj_pool, proj_attn=proj_attn, proj_rwkv=proj_rwkv, w_out=w_out,
             ffn_w1=ffn_w1, ffn_w3=ffn_w3, ffn_w2=ffn_w2, router_w=router_w, router_b=router_b,
             moe_w1=moe_w1, moe_w3=moe_w3, moe_w2=moe_w2)
    P = _prep_weights(W)
    n_prompt = c_prompt.shape[0]
    mod_all = _ada_mod(jnp.concatenate([c_prompt, c_sample], axis=0), ada_w, ada_b)
    y_prompt, p_new = _trunk(x_prompt, mod_all[:, :n_prompt], P, 0, None, RWKV_TILE_CHUNKS)
    y_sample, s_new = _trunk(x_sample, mod_all[:, n_prompt:], P, PAST_LEN,
                             (cache_attn_k, cache_attn_v, state_pool, state_rwkv_shift, state_rwkv_wkv), 1)
    return (y_prompt, y_sample, *p_new, *s_new)
```
